```python
import math
import jax, jax.numpy as jnp
from jax import lax
import numpy as np

D_MODEL = 1024
BATCH = 8
SEQ = 8192
DEPTH = 4
DEC_BATCH = 2
DEC_SEQ = 8192
PAST_LEN = 128

N_MIXERS = 4
HEAD_DIM = 64
D_FF = 4 * D_MODEL
BLOCK = 128
EPS = 1e-6
NEG = -1e30
ADA_INIT = 0.5
A_HEADS = D_MODEL // HEAD_DIM
A_KV_HEADS = 4
A_WINDOW = 128
B_HEADS = D_MODEL // HEAD_DIM
GRID_W = 64
WIN_R = 8
WIN_C = 16
C_HEADS = D_MODEL // (2 * HEAD_DIM)
D_GROUPS = ((128, 1), (512, 4), (2048, 16))
D_HEADS = D_MODEL // (2 * HEAD_DIM)

kernel_name = 'hybrid_bidir_encoder'

F32 = jnp.float32


def rmsnorm(x, g):
    xf = x.astype(F32)
    y = xf * lax.rsqrt(jnp.mean(xf * xf, axis=-1, keepdims=True) + EPS)
    return (y * g.astype(F32)).astype(x.dtype)


def alibi_slopes(n):
    return 2.0 ** (-8.0 * jnp.arange(1, n + 1, dtype=F32) / n)


def mixer_window_gqa(h, wqkv, wo, sink):
    B_, S, _ = h.shape
    nb = S // BLOCK
    grp = A_HEADS // A_KV_HEADS
    qkv = h @ wqkv
    q = qkv[..., :A_HEADS * HEAD_DIM].reshape(B_, S, A_KV_HEADS, grp, HEAD_DIM)
    k = qkv[..., A_HEADS * HEAD_DIM:(A_HEADS + A_KV_HEADS) * HEAD_DIM].reshape(B_, S, A_KV_HEADS, HEAD_DIM)
    v = qkv[..., (A_HEADS + A_KV_HEADS) * HEAD_DIM:].reshape(B_, S, A_KV_HEADS, HEAD_DIM)
    pad = ((0, 0), (A_WINDOW, A_WINDOW), (0, 0), (0, 0))
    kp = jnp.pad(k, pad)
    vp = jnp.pad(v, pad)
    span = BLOCK + 2 * A_WINDOW
    rel = jnp.arange(BLOCK)[:, None] + A_WINDOW - jnp.arange(span)[None, :]
    dist = jnp.abs(rel)
    slopes = alibi_slopes(A_HEADS).reshape(A_KV_HEADS, grp)[:, :, None, None]
    sink_f = sink.astype(F32).reshape(A_KV_HEADS, grp)[:, :, None, None]
    scale = HEAD_DIM ** -0.5

    def block(bi):
        q0 = bi * BLOCK
        qb = lax.dynamic_slice_in_dim(q, q0, BLOCK, axis=1)
        kb = lax.dynamic_slice_in_dim(kp, q0, span, axis=1)
        vb = lax.dynamic_slice_in_dim(vp, q0, span, axis=1)
        s = jnp.einsum('bikge,bjke->bkgij', qb, kb, preferred_element_type=F32) * scale
        s = s - slopes * dist.astype(F32)
        key_pos = q0 - A_WINDOW + jnp.arange(span)
        valid = (dist <= A_WINDOW) & ((key_pos >= 0) & (key_pos < S))[None, :]
        s = jnp.where(valid, s, NEG)
        mx = jnp.maximum(jnp.max(s, axis=-1, keepdims=True), sink_f)
        p = jnp.exp(s - mx)
        den = jnp.sum(p, axis=-1, keepdims=True) + jnp.exp(sink_f - mx)
        o = jnp.einsum('bkgij,bjke->bikge', p / den, vb.astype(F32))
        return o.astype(h.dtype)

    o = lax.map(block, jnp.arange(nb))
    o = jnp.moveaxis(o, 0, 1).reshape(B_, S, A_HEADS * HEAD_DIM)
    return o @ wo


def mixer_neighborhood(h, wqkv, wo, rel_bias):
    B_, S, _ = h.shape
    rows = S // GRID_W
    wr = min(WIN_R, rows)
    wc = min(WIN_C, GRID_W)
    qkv = (h @ wqkv).reshape(B_, rows, GRID_W, 3, B_HEADS, HEAD_DIM)
    q = qkv[:, :, :, 0]
    k = qkv[:, :, :, 1]
    v = qkv[:, :, :, 2]
    col = jnp.arange(GRID_W)
    cs = jnp.clip(col - wc // 2, 0, GRID_W - wc)
    col_valid = (col[None, :] >= cs[:, None]) & (col[None, :] < cs[:, None] + wc)
    dc_idx = jnp.clip(col[None, :] - col[:, None] + WIN_C - 1, 0, 2 * WIN_C - 2)
    bias_tab = rel_bias.astype(F32)
    scale = HEAD_DIM ** -0.5

    def row(r):
        rs = jnp.clip(r - wr // 2, 0, rows - wr)
        qr = lax.dynamic_index_in_dim(q, r, axis=1, keepdims=False)
        kr = lax.dynamic_slice_in_dim(k, rs, wr, axis=1)
        vr = lax.dynamic_slice_in_dim(v, rs, wr, axis=1)
        dr_idx = rs + jnp.arange(wr) - r + WIN_R - 1
        bias = bias_tab[:, dr_idx[None, :, None], dc_idx[:, None, :]]
        s = jnp.einsum('bqhe,bakhe->bhqak', qr, kr, preferred_element_type=F32) * scale + bias
        s = jnp.where(col_valid[:, None, :], s, NEG)
        p = jax.nn.softmax(s.reshape(B_, B_HEADS, GRID_W, wr * GRID_W), axis=-1)
        o = jnp.einsum('bhqn,bnhe->bqhe', p, vr.reshape(B_, wr * GRID_W, B_HEADS, HEAD_DIM).astype(F32))
        return o.astype(h.dtype)

    o = lax.map(row, jnp.arange(rows))
    o = jnp.moveaxis(o, 0, 1).reshape(B_, S, B_HEADS * HEAD_DIM)
    return o @ wo


def mixer_diff(h, wqkv, wo, lam, subln, lam_init):
    B_, S, _ = h.shape
    nb = S // BLOCK
    e = HEAD_DIM
    qkv = h @ wqkv
    q = qkv[..., :2 * C_HEADS * e].reshape(B_, S, C_HEADS, 2, e)
    k = qkv[..., 2 * C_HEADS * e:4 * C_HEADS * e].reshape(B_, S, C_HEADS, 2, e)
    vf = qkv[..., 4 * C_HEADS * e:].reshape(B_, S, C_HEADS, 2 * e).astype(F32)
    lf = lam.astype(F32)
    lam_full = jnp.exp(jnp.sum(lf[0] * lf[1])) - jnp.exp(jnp.sum(lf[2] * lf[3])) + lam_init
    slopes = alibi_slopes(C_HEADS)[:, None, None, None]
    key_pos = jnp.arange(S)
    scale = e ** -0.5

    def block(bi):
        q0 = bi * BLOCK
        qb = lax.dynamic_slice_in_dim(q, q0, BLOCK, axis=1)
        s = jnp.einsum('bihce,bjhce->bhcij', qb, k, preferred_element_type=F32) * scale
        dist = jnp.abs(q0 + jnp.arange(BLOCK)[:, None] - key_pos[None, :]).astype(F32)
        a = jax.nn.softmax(s - slopes * dist, axis=-1)
        w = a[:, :, 0] - lam_full * a[:, :, 1]
        return jnp.einsum('bhij,bjhe->bihe', w, vf)

    o = lax.map(block, jnp.arange(nb))
    o = jnp.moveaxis(o, 0, 1).reshape(B_, S, C_HEADS, 2 * e)
    o = rmsnorm(o, subln) * (1.0 - lam_init)
    return o.reshape(B_, S, C_HEADS * 2 * e).astype(h.dtype) @ wo


def mixer_dilated(h, wqkv, wo):
    B_, S, _ = h.shape
    nb = S // BLOCK
    e = HEAD_DIM
    G = len(D_GROUPS)
    qkv = (h @ wqkv).reshape(B_, S, G, 3, D_HEADS, e)
    q = qkv[:, :, :, 0]
    kps, vps = [], []
    for g, (w, d) in enumerate(D_GROUPS):
        pad = ((0, 0), (w // 2, w // 2), (0, 0), (0, 0))
        kps.append(jnp.pad(qkv[:, :, g, 1], pad))
        vps.append(jnp.pad(qkv[:, :, g, 2], pad))
    slopes = alibi_slopes(D_HEADS)[:, None, None, None]
    scale = e ** -0.5

    def block(bi):
        q0 = bi * BLOCK
        qall = lax.dynamic_slice_in_dim(q, q0, BLOCK, axis=1)
        outs, lses = [], []
        for g, (w, d) in enumerate(D_GROUPS):
            n = w // (2 * d)
            half = w // 2
            nq = BLOCK // d
            nk = nq + 2 * n
            qb = qall[:, :, g].reshape(B_, nq, d, D_HEADS, e)
            kb = lax.dynamic_slice_in_dim(kps[g], q0, BLOCK + w, axis=1).reshape(B_, nk, d, D_HEADS, e)
            vb = lax.dynamic_slice_in_dim(vps[g], q0, BLOCK + w, axis=1).reshape(B_, nk, d, D_HEADS, e)
            s = jnp.einsum('birhe,bmrhe->bhrim', qb, kb, preferred_element_type=F32) * scale
            off = jnp.arange(nk)[None, :] - jnp.arange(nq)[:, None] - n
            key_pos = q0 - half + jnp.arange(nk)[None, None, :] * d + jnp.arange(d)[:, None, None]
            valid = (jnp.abs(off) <= n)[None] & (key_pos >= 0) & (key_pos < S)
            s = s - slopes * (jnp.abs(off) * d).astype(F32)
            s = jnp.where(valid, s, NEG)
            mx = jnp.max(s, axis=-1, keepdims=True)
            p = jnp.exp(s - mx)
            den = jnp.sum(p, axis=-1, keepdims=True)
            o = jnp.einsum('bhrim,bmrhe->birhe', p / den, vb.astype(F32)).reshape(B_, BLOCK, D_HEADS, e)
            lse = jnp.transpose((mx + jnp.log(den))[..., 0], (0, 3, 2, 1)).reshape(B_, BLOCK, D_HEADS)
            outs.append(o)
            lses.append(lse)
        wts = jax.nn.softmax(jnp.stack(lses, axis=0), axis=0)
        return jnp.sum(wts[..., None] * jnp.stack(outs, axis=0), axis=0).astype(h.dtype)

    o = lax.map(block, jnp.arange(nb))
    o = jnp.moveaxis(o, 0, 1).reshape(B_, S, D_HEADS * e)
    return o @ wo


def trunk(x, c, ada_w, ada_b, g_mix_pre, g_mix_post, g_mlp_pre, g_mlp_post, mlp_w1, mlp_w2,
          a_wqkv, a_wo, a_sink, b_wqkv, b_wo, b_rel_bias, c_wqkv, c_wo, c_lambda, c_subln, d_wqkv, d_wo):
    c_act = jax.nn.silu(c)
    for i in range(DEPTH):
        mod = (c_act @ ada_w[i] + ada_b[i])[:, None, :]
        sh1, sc1, gt1, sh2, sc2, gt2 = jnp.split(mod, 6, axis=-1)
        h = rmsnorm(x, g_mix_pre[i]) * (1 + sc1) + sh1
        kind, occ = i % N_MIXERS, i // N_MIXERS
        if kind == 0:
            h = mixer_window_gqa(h, a_wqkv[occ], a_wo[occ], a_sink[occ])
        elif kind == 1:
            h = mixer_neighborhood(h, b_wqkv[occ], b_wo[occ], b_rel_bias[occ])
        elif kind == 2:
            h = mixer_diff(h, c_wqkv[occ], c_wo[occ], c_lambda[occ], c_subln[occ],
                           0.8 - 0.6 * math.exp(-0.3 * i))
        else:
            h = mixer_dilated(h, d_wqkv[occ], d_wo[occ])
        x = x + gt1 * rmsnorm(h, g_mix_post[i])
        h = rmsnorm(x, g_mlp_pre[i]) * (1 + sc2) + sh2
        h = jnp.square(jax.nn.relu(h @ mlp_w1[i])) @ mlp_w2[i]
        x = x + gt2 * rmsnorm(h, g_mlp_post[i])
    return x


def setup_inputs(seed: int = 0) -> dict:
    key = jax.random.key(seed)
    keys = jax.random.split(key, 24)

    def nrm(i, shape, scale):
        return jax.random.normal(keys[i], shape, jnp.float32) * scale

    n_a, n_b, n_c, n_d = [len(range(m, DEPTH, N_MIXERS)) for m in range(N_MIXERS)]
    D = D_MODEL
    return {
        'x_prompt': nrm(0, (BATCH, SEQ, D), 1.0),
        'x_sample': nrm(1, (DEC_BATCH, DEC_SEQ, D), 1.0),
        'c_prompt': nrm(2, (BATCH, D), 1.0),
        'c_sample': nrm(3, (DEC_BATCH, D), 1.0),
        'ada_w': nrm(4, (DEPTH, D, 6 * D), ADA_INIT * D ** -0.5),
        'ada_b': nrm(5, (DEPTH, 6 * D), 0.02),
        'g_mix_pre': 1.0 + nrm(6, (DEPTH, D), 0.02),
        'g_mix_post': 1.0 + nrm(7, (DEPTH, D), 0.02),
        'g_mlp_pre': 1.0 + nrm(8, (DEPTH, D), 0.02),
        'g_mlp_post': 1.0 + nrm(9, (DEPTH, D), 0.02),
        'mlp_w1': nrm(10, (DEPTH, D, D_FF), D ** -0.5),
        'mlp_w2': nrm(11, (DEPTH, D_FF, D), D_FF ** -0.5),
        'a_wqkv': nrm(12, (n_a, D, (A_HEADS + 2 * A_KV_HEADS) * HEAD_DIM), D ** -0.5),
        'a_wo': nrm(13, (n_a, A_HEADS * HEAD_DIM, D), (A_HEADS * HEAD_DIM) ** -0.5),
        'a_sink': nrm(14, (n_a, A_HEADS), 1.0),
        'b_wqkv': nrm(15, (n_b, D, 3 * B_HEADS * HEAD_DIM), D ** -0.5),
        'b_wo': nrm(16, (n_b, B_HEADS * HEAD_DIM, D), (B_HEADS * HEAD_DIM) ** -0.5),
        'b_rel_bias': nrm(17, (n_b, B_HEADS, 2 * WIN_R - 1, 2 * WIN_C - 1), 0.5),
        'c_wqkv': nrm(18, (n_c, D, 6 * C_HEADS * HEAD_DIM), D ** -0.5),
        'c_wo': nrm(19, (n_c, 2 * C_HEADS * HEAD_DIM, D), (2 * C_HEADS * HEAD_DIM) ** -0.5),
        'c_lambda': nrm(20, (n_c, 4, HEAD_DIM), 0.1),
        'c_subln': 1.0 + nrm(21, (n_c, 2 * HEAD_DIM), 0.02),
        'd_wqkv': nrm(22, (n_d, D, len(D_GROUPS) * 3 * D_HEADS * HEAD_DIM), D ** -0.5),
        'd_wo': nrm(23, (n_d, D_HEADS * HEAD_DIM, D), (D_HEADS * HEAD_DIM) ** -0.5),
    }


def reference(x_prompt, x_sample, c_prompt, c_sample, ada_w, ada_b, g_mix_pre, g_mix_post, g_mlp_pre,
              g_mlp_post, mlp_w1, mlp_w2, a_wqkv, a_wo, a_sink, b_wqkv, b_wo, b_rel_bias, c_wqkv, c_wo,
              c_lambda, c_subln, d_wqkv, d_wo):
    y_prompt = trunk(x_prompt, c_prompt, ada_w, ada_b, g_mix_pre, g_mix_post, g_mlp_pre, g_mlp_post,
                     mlp_w1, mlp_w2, a_wqkv, a_wo, a_sink, b_wqkv, b_wo, b_rel_bias, c_wqkv, c_wo,
                     c_lambda, c_subln, d_wqkv, d_wo)
    y_sample = trunk(x_sample, c_sample, ada_w, ada_b, g_mix_pre, g_mix_post, g_mlp_pre, g_mlp_post,
                     mlp_w1, mlp_w2, a_wqkv, a_wo, a_sink, b_wqkv, b_wo, b_rel_bias, c_wqkv, c_wo,
                     c_lambda, c_subln, d_wqkv, d_wo)
    return (y_prompt, y_sample)
```

```python
import functools
import math

import jax
import jax.numpy as jnp
from jax import lax
from jax.experimental import pallas as pl
from jax.experimental.pallas import tpu as pltpu

F32 = jnp.float32
BF16 = jnp.bfloat16

D_MODEL = 1024
HEAD_DIM = 64
LANES = 128
D_FF = 4 * D_MODEL
EPS = 1e-6
NEG = -1e30
SCALE = HEAD_DIM ** -0.5
N_MIXERS = 4

A_HEADS, A_KV_HEADS, A_WINDOW = 16, 4, 128
B_HEADS, GRID_W, WIN_R, WIN_C = 16, 64, 8, 16
C_HEADS = 8
D_GROUPS = ((128, 1), (512, 4), (2048, 16))
D_HEADS = 8

VMEM_LIMIT_BYTES = 56 * 1024 * 1024

_NT = (((1,), (1,)), ((), ()))


def _cparams(*sem):
    return pltpu.CompilerParams(dimension_semantics=sem, vmem_limit_bytes=VMEM_LIMIT_BYTES)


def _smem_spec():
    return pl.BlockSpec(memory_space=pltpu.SMEM)


def _alibi_slopes(n):
    return 2.0 ** (-8.0 * jnp.arange(1, n + 1, dtype=F32) / n)


def _modulated_norm(x, g, sc, sh):
    ms = jnp.mean(x * x, axis=-1, keepdims=True)
    return (x * lax.rsqrt(ms + EPS) * g) * (1.0 + sc) + sh


def _gated_norm_residual(x, h, g, gt):
    ms = jnp.mean(h * h, axis=-1, keepdims=True)
    return x + gt * (h * lax.rsqrt(ms + EPS) * g)


def _adaln_kernel(c_ref, w_ref, b_ref, o_ref):
    c = c_ref[...]
    ca = (c * jax.nn.sigmoid(c)).astype(BF16)
    o_ref[0] = jnp.dot(ca, w_ref[0].astype(BF16), preferred_element_type=F32) + b_ref[0]


def _adaln_call(c_pad, ada_w, ada_b):
    depth, d, n = ada_w.shape
    bp = c_pad.shape[0]
    tn = 1536
    return pl.pallas_call(
        _adaln_kernel,
        grid=(depth, n // tn),
        in_specs=[
            pl.BlockSpec((bp, d), lambda l, j: (0, 0)),
            pl.BlockSpec((1, d, tn), lambda l, j: (l, 0, j)),
            pl.BlockSpec((1, 1, tn), lambda l, j: (l, 0, j)),
        ],
        out_specs=pl.BlockSpec((1, bp, tn), lambda l, j: (l, 0, j)),
        out_shape=jax.ShapeDtypeStruct((depth, bp, n), F32),
        compiler_params=_cparams("parallel", "parallel"),
        name="adaln",
    )(c_pad, ada_w, ada_b.reshape(depth, 1, n))


def _qkv_kernel(*refs, n_chunk, has_t):
    if has_t:
        x_ref, g_ref, sc_ref, sh_ref, w_ref, wt_ref, o_ref, ot_ref = refs
    else:
        x_ref, g_ref, sc_ref, sh_ref, w_ref, o_ref = refs
    h = _modulated_norm(x_ref[0], g_ref[...], sc_ref[0], sh_ref[0]).astype(BF16)
    n = w_ref.shape[1]
    for c0 in range(0, n, n_chunk):
        o_ref[0, :, c0:c0 + n_chunk] = jnp.dot(
            h, w_ref[:, c0:c0 + n_chunk], preferred_element_type=F32).astype(o_ref.dtype)
    if has_t:
        ot_ref[0] = lax.dot_general(wt_ref[...], h, _NT, preferred_element_type=F32).astype(ot_ref.dtype)


def _qkv_call(x, g, sc, sh, w, wt=None):
    b, s, d = x.shape
    n = w.shape[1]
    tm = min(256, s)
    in_specs = [
        pl.BlockSpec((1, tm, d), lambda bi, i: (bi, i, 0)),
        pl.BlockSpec((1, d), lambda bi, i: (0, 0)),
        pl.BlockSpec((1, 1, d), lambda bi, i: (bi, 0, 0)),
        pl.BlockSpec((1, 1, d), lambda bi, i: (bi, 0, 0)),
        pl.BlockSpec((d, n), lambda bi, i: (0, 0)),
    ]
    out_shape = [jax.ShapeDtypeStruct((b, s, n), BF16)]
    out_specs = [pl.BlockSpec((1, tm, n), lambda bi, i: (bi, i, 0))]
    args = [x, g.reshape(1, d), sc, sh, w]
    if wt is not None:
        nt = wt.shape[0]
        in_specs.append(pl.BlockSpec((nt, d), lambda bi, i: (0, 0)))
        out_shape.append(jax.ShapeDtypeStruct((b, nt, s), BF16))
        out_specs.append(pl.BlockSpec((1, nt, tm), lambda bi, i: (bi, 0, i)))
        args.append(wt)
    outs = pl.pallas_call(
        functools.partial(_qkv_kernel, n_chunk=512, has_t=wt is not None),
        grid=(b, s // tm),
        in_specs=in_specs,
        out_specs=out_specs,
        out_shape=out_shape,
        compiler_params=_cparams("parallel", "parallel"),
        name="qkv_proj",
    )(*args)
    return outs if wt is not None else outs[0]


def _out_kernel(o_ref, w_ref, x_ref, gt_ref, g_ref, y_ref):
    h = jnp.dot(o_ref[0], w_ref[...], preferred_element_type=F32)
    y_ref[0] = _gated_norm_residual(x_ref[0], h, g_ref[...], gt_ref[0])


def _out_call(o, w, x, gt, g):
    b, s, d = x.shape
    kd = o.shape[-1]
    tm = min(512, s)
    return pl.pallas_call(
        _out_kernel,
        grid=(b, s // tm),
        in_specs=[
            pl.BlockSpec((1, tm, kd), lambda bi, i: (bi, i, 0)),
            pl.BlockSpec((kd, d), lambda bi, i: (0, 0)),
            pl.BlockSpec((1, tm, d), lambda bi, i: (bi, i, 0)),
            pl.BlockSpec((1, 1, d), lambda bi, i: (bi, 0, 0)),
            pl.BlockSpec((1, d), lambda bi, i: (0, 0)),
        ],
        out_specs=pl.BlockSpec((1, tm, d), lambda bi, i: (bi, i, 0)),
        out_shape=jax.ShapeDtypeStruct((b, s, d), F32),
        compiler_params=_cparams("parallel", "parallel"),
        name="out_proj",
    )(o, w, x, gt, g.reshape(1, d))


def _mlp_kernel(x_ref, g1_ref, sc_ref, sh_ref, w1_ref, w2_ref, gt_ref, g2_ref, y_ref, *, f_chunk):
    x = x_ref[0]
    h = _modulated_norm(x, g1_ref[...], sc_ref[0], sh_ref[0]).astype(BF16)
    acc = jnp.zeros(x.shape, F32)
    for c0 in range(0, w1_ref.shape[1], f_chunk):
        a = jnp.dot(h, w1_ref[:, c0:c0 + f_chunk], preferred_element_type=F32)
        a = jnp.maximum(a, 0.0)
        a = (a * a).astype(BF16)
        acc = acc + jnp.dot(a, w2_ref[c0:c0 + f_chunk, :], preferred_element_type=F32)
    y_ref[0] = _gated_norm_residual(x, acc, g2_ref[...], gt_ref[0])


def _mlp_call(x, g1, sc, sh, w1, w2, gt, g2):
    b, s, d = x.shape
    f = w1.shape[1]
    tm = min(512, s)
    vec = pl.BlockSpec((1, 1, d), lambda bi, i: (bi, 0, 0))
    gain = pl.BlockSpec((1, d), lambda bi, i: (0, 0))
    return pl.pallas_call(
        functools.partial(_mlp_kernel, f_chunk=1024),
        grid=(b, s // tm),
        in_specs=[
            pl.BlockSpec((1, tm, d), lambda bi, i: (bi, i, 0)),
            gain, vec, vec,
            pl.BlockSpec((d, f), lambda bi, i: (0, 0), pipeline_mode=pl.Buffered(1)),
            pl.BlockSpec((f, d), lambda bi, i: (0, 0), pipeline_mode=pl.Buffered(1)),
            vec, gain,
        ],
        out_specs=pl.BlockSpec((1, tm, d), lambda bi, i: (bi, i, 0)),
        out_shape=jax.ShapeDtypeStruct((b, s, d), F32),
        compiler_params=_cparams("parallel", "parallel"),
        name="mlp",
    )(x, g1.reshape(1, d), sc, sh, w1, w2, gt, g2.reshape(1, d))


def _head_select(q, lane, hh):
    keep = (lane >= HEAD_DIM) if hh else (lane < HEAD_DIM)
    return jnp.where(keep, q, jnp.zeros_like(q))


def _banded_kernel(slopes_ref, sink_ref, *refs, groups, tq, seq, use_sink):
    ng = len(groups)
    q_refs, k_refs, v_refs = refs[:ng], refs[ng:2 * ng], refs[2 * ng:3 * ng]
    o_ref = refs[3 * ng]
    p = pl.program_id(1)
    q0 = pl.program_id(2) * tq
    lane = lax.broadcasted_iota(jnp.int32, (tq, LANES), 1)

    wins = []
    for g, (half, dil) in enumerate(groups):
        span = tq + 2 * half
        start = pl.multiple_of(jnp.clip(q0 - half, 0, seq - span), HEAD_DIM)
        kw = k_refs[g][0, pl.ds(start, span), :]
        vw = v_refs[g][0, pl.ds(start, span), :]
        row = lax.broadcasted_iota(jnp.int32, (tq, span), 0)
        col = lax.broadcasted_iota(jnp.int32, (tq, span), 1)
        delta = col - row + (start - q0)
        dist = jnp.abs(delta)
        valid = dist <= half
        if dil > 1:
            valid = valid & ((delta & (dil - 1)) == 0)
        wins.append((kw, vw, dist.astype(F32), valid))

    res = []
    for hh in range(2):
        slope = slopes_ref[2 * p + hh]
        scores = []
        m = None
        for g in range(ng):
            kw, _, dist, valid = wins[g]
            qm = _head_select(q_refs[g][0], lane, hh)
            s = lax.dot_general(qm, kw, _NT, preferred_element_type=F32) * SCALE
            s = jnp.where(valid, s - slope * dist, NEG)
            scores.append(s)
            mg = jnp.max(s, axis=-1, keepdims=True)
            m = mg if m is None else jnp.maximum(m, mg)
        if use_sink:
            sink = sink_ref[2 * p + hh]
            m = jnp.maximum(m, sink)
        den = jnp.exp(sink - m) if use_sink else jnp.zeros_like(m)
        acc = jnp.zeros((tq, LANES), F32)
        for g in range(ng):
            pr = jnp.exp(scores[g] - m)
            den = den + jnp.sum(pr, axis=-1, keepdims=True)
            acc = acc + jnp.dot(pr.astype(BF16), wins[g][1], preferred_element_type=F32)
        res.append(acc / den)
    o_ref[0] = jnp.where(lane < HEAD_DIM, res[0], res[1]).astype(o_ref.dtype)


def _banded_call(qkv, q_blk, k_blk, v_blk, groups, n_tiles, slopes, sinks, use_sink):
    b, s, _ = qkv.shape
    tq = 128
    ng = len(groups)

    def qspec(fn):
        return pl.BlockSpec((1, tq, LANES), lambda bi, p, i: (bi, i, fn(p)))

    def kvspec(fn):
        return pl.BlockSpec((1, s, LANES), lambda bi, p, i: (bi, 0, fn(p)))

    in_specs = [_smem_spec(), _smem_spec()]
    in_specs += [qspec(f) for f in q_blk] + [kvspec(f) for f in k_blk] + [kvspec(f) for f in v_blk]
    return pl.pallas_call(
        functools.partial(_banded_kernel, groups=groups, tq=tq, seq=s, use_sink=use_sink),
        grid=(b, n_tiles, s // tq),
        in_specs=in_specs,
        out_specs=pl.BlockSpec((1, tq, LANES), lambda bi, p, i: (bi, i, p)),
        out_shape=jax.ShapeDtypeStruct((b, s, n_tiles * LANES), BF16),
        compiler_params=_cparams("parallel", "parallel", "arbitrary"),
        name="banded_attn",
    )(slopes, sinks, *([qkv] * (3 * ng)))


def _nbr_kernel(q_ref, k_ref, v_ref, t_ref, o_ref, *, rows_per_tile, n_rows):
    i = pl.program_id(2)
    nkeys = WIN_R * GRID_W
    lane = lax.broadcasted_iota(jnp.int32, (GRID_W, LANES), 1)
    for rr in range(rows_per_tile):
        r = i * rows_per_tile + rr
        rs = jnp.clip(r - WIN_R // 2, 0, n_rows - WIN_R)
        off = rs - r + (WIN_R - 1)
        start = pl.multiple_of(rs * GRID_W, GRID_W)
        kw = k_ref[0, pl.ds(start, nkeys), :]
        vw = v_ref[0, pl.ds(start, nkeys), :]
        q = q_ref[0, rr * GRID_W:(rr + 1) * GRID_W, :]
        res = []
        for hh in range(2):
            qm = _head_select(q, lane, hh)
            s = lax.dot_general(qm, kw, _NT, preferred_element_type=F32) * SCALE
            s = s + t_ref[0, hh, off]
            m = jnp.max(s, axis=-1, keepdims=True)
            pr = jnp.exp(s - m)
            den = jnp.sum(pr, axis=-1, keepdims=True)
            res.append(jnp.dot(pr.astype(BF16), vw, preferred_element_type=F32) / den)
        o_ref[0, rr * GRID_W:(rr + 1) * GRID_W, :] = jnp.where(
            lane < HEAD_DIM, res[0], res[1]).astype(o_ref.dtype)


def _nbr_bias_table(rel_bias):
    col = jnp.arange(GRID_W)
    cs = jnp.clip(col - WIN_C // 2, 0, GRID_W - WIN_C)
    col_valid = (col[None, :] >= cs[:, None]) & (col[None, :] < cs[:, None] + WIN_C)
    dc_idx = jnp.clip(col[None, :] - col[:, None] + WIN_C - 1, 0, 2 * WIN_C - 2)
    dr = jnp.arange(WIN_R)[:, None] + jnp.arange(WIN_R)[None, :]
    tab = rel_bias.astype(F32)[:, dr]
    tab = tab[..., dc_idx]
    tab = jnp.where(col_valid, tab, NEG)
    tab = jnp.transpose(tab, (0, 1, 3, 2, 4))
    h = tab.shape[0]
    return tab.reshape(h // 2, 2, WIN_R, GRID_W, WIN_R * GRID_W)


def _nbr_call(qkv, table):
    b, s, _ = qkv.shape
    n_tiles = B_HEADS // 2
    rows_per_tile = 2
    tq = rows_per_tile * GRID_W
    n_rows = s // GRID_W
    kv = lambda off: pl.BlockSpec((1, s, LANES), lambda bi, p, i: (bi, 0, off + p))
    return pl.pallas_call(
        functools.partial(_nbr_kernel, rows_per_tile=rows_per_tile, n_rows=n_rows),
        grid=(b, n_tiles, s // tq),
        in_specs=[
            pl.BlockSpec((1, tq, LANES), lambda bi, p, i: (bi, i, p)),
            kv(n_tiles), kv(2 * n_tiles),
            pl.BlockSpec((1, 2, WIN_R, GRID_W, WIN_R * GRID_W), lambda bi, p, i: (p, 0, 0, 0, 0)),
        ],
        out_specs=pl.BlockSpec((1, tq, LANES), lambda bi, p, i: (bi, i, p)),
        out_shape=jax.ShapeDtypeStruct((b, s, n_tiles * LANES), BF16),
        compiler_params=_cparams("parallel", "parallel", "arbitrary"),
        name="nbr_attn",
    )(qkv, qkv, qkv, table)


def _diff_kernel(slopes_ref, lam_ref, q_ref, k_ref, vt_ref, g_ref, o_ref,
                 qm_s, m_s, l_s, acc_s, *, tq, tk, post_scale):
    h = pl.program_id(1)
    i = pl.program_id(2)
    j = pl.program_id(3)

    @pl.when(j == 0)
    def _():
        q = q_ref[0] * SCALE
        lane = lax.broadcasted_iota(jnp.int32, (tq, LANES), 1)
        qm_s[0] = _head_select(q, lane, 0)
        qm_s[1] = _head_select(q, lane, 1)
        m_s[...] = jnp.full(m_s.shape, NEG, F32)
        l_s[...] = jnp.zeros(l_s.shape, F32)
        acc_s[...] = jnp.zeros(acc_s.shape, F32)

    k = k_ref[0]
    vt = vt_ref[0]
    kpos = lax.broadcasted_iota(jnp.int32, (tk, tq), 0) + j * tk
    qpos = lax.broadcasted_iota(jnp.int32, (tk, tq), 1) + i * tq
    bias = slopes_ref[h] * jnp.abs(kpos - qpos).astype(F32)
    for c in range(2):
        s = lax.dot_general(k, qm_s[c], _NT, preferred_element_type=F32) - bias
        m_old = m_s[c]
        m_new = jnp.maximum(m_old, jnp.max(s, axis=0, keepdims=True))
        alpha = jnp.exp(m_old - m_new)
        pr = jnp.exp(s - m_new)
        l_s[c] = alpha * l_s[c] + jnp.sum(pr, axis=0, keepdims=True)
        acc_s[c] = alpha * acc_s[c] + jnp.dot(vt, pr.astype(BF16), preferred_element_type=F32)
        m_s[c] = m_new

    @pl.when(j == pl.num_programs(3) - 1)
    def _():
        o = acc_s[0] / l_s[0] - lam_ref[0] * (acc_s[1] / l_s[1])
        ms = jnp.mean(o * o, axis=0, keepdims=True)
        y = (o * lax.rsqrt(ms + EPS) * g_ref[...]) * post_scale
        o_ref[0] = y.T.astype(o_ref.dtype)


def _diff_call(qk, vt, slopes, lam, subln, post_scale):
    b, s, _ = qk.shape
    tq = min(512, s)
    tk = min(512, s)
    return pl.pallas_call(
        functools.partial(_diff_kernel, tq=tq, tk=tk, post_scale=post_scale),
        grid=(b, C_HEADS, s // tq, s // tk),
        in_specs=[
            _smem_spec(), _smem_spec(),
            pl.BlockSpec((1, tq, LANES), lambda bi, h, i, j: (bi, i, h)),
            pl.BlockSpec((1, tk, LANES), lambda bi, h, i, j: (bi, j, C_HEADS + h)),
            pl.BlockSpec((1, LANES, tk), lambda bi, h, i, j: (bi, h, j)),
            pl.BlockSpec((LANES, 1), lambda bi, h, i, j: (0, 0)),
        ],
        out_specs=pl.BlockSpec((1, tq, LANES), lambda bi, h, i, j: (bi, i, h)),
        out_shape=jax.ShapeDtypeStruct((b, s, C_HEADS * LANES), BF16),
        scratch_shapes=[
            pltpu.VMEM((2, tq, LANES), BF16),
            pltpu.VMEM((2, 1, tq), F32),
            pltpu.VMEM((2, 1, tq), F32),
            pltpu.VMEM((2, LANES, tq), F32),
        ],
        compiler_params=_cparams("parallel", "parallel", "parallel", "arbitrary"),
        name="diff_attn",
    )(slopes, lam, qk, qk, vt, subln.astype(F32).reshape(LANES, 1))


def _mixer_a(x, g, sc, sh, wqkv, sink):
    d = x.shape[-1]
    nq = A_HEADS * HEAD_DIM
    nkv = A_KV_HEADS * HEAD_DIM
    wq = wqkv[:, :nq]

    def dup(wpart):
        wpart = wpart.reshape(d, A_KV_HEADS, 1, HEAD_DIM)
        return jnp.broadcast_to(wpart, (d, A_KV_HEADS, 2, HEAD_DIM)).reshape(d, 2 * nkv)

    w = jnp.concatenate([wq, dup(wqkv[:, nq:nq + nkv]), dup(wqkv[:, nq + nkv:])], axis=1).astype(BF16)
    qkv = _qkv_call(x, g, sc, sh, w)
    n_tiles = A_HEADS // 2
    q_tiles = nq // LANES
    kv_tiles = 2 * nkv // LANES
    pairs_per_kv = A_HEADS // A_KV_HEADS // 2
    return _banded_call(
        qkv,
        q_blk=[lambda p: p],
        k_blk=[lambda p: q_tiles + p // pairs_per_kv],
        v_blk=[lambda p: q_tiles + kv_tiles + p // pairs_per_kv],
        groups=((A_WINDOW, 1),), n_tiles=n_tiles,
        slopes=_alibi_slopes(A_HEADS), sinks=sink.astype(F32), use_sink=True)


def _mixer_b(x, g, sc, sh, wqkv, rel_bias):
    qkv = _qkv_call(x, g, sc, sh, wqkv.astype(BF16))
    return _nbr_call(qkv, _nbr_bias_table(rel_bias))


def _mixer_c(x, g, sc, sh, wqkv, lam, subln, lam_init):
    nqk = 4 * C_HEADS * HEAD_DIM
    qk, vt = _qkv_call(x, g, sc, sh, wqkv[:, :nqk].astype(BF16), wqkv[:, nqk:].T.astype(BF16))
    lf = lam.astype(F32)
    lam_full = jnp.exp(jnp.sum(lf[0] * lf[1])) - jnp.exp(jnp.sum(lf[2] * lf[3])) + lam_init
    return _diff_call(qk, vt, _alibi_slopes(C_HEADS), lam_full.reshape(1), subln, 1.0 - lam_init)


def _mixer_d(x, g, sc, sh, wqkv):
    qkv = _qkv_call(x, g, sc, sh, wqkv.astype(BF16))
    n_tiles = D_HEADS // 2
    per_group = 3 * n_tiles
    blk = lambda t: [(lambda p, gi=gi: gi * per_group + t * n_tiles + p) for gi in range(len(D_GROUPS))]
    groups = tuple((w // 2, dil) for w, dil in D_GROUPS)
    return _banded_call(
        qkv, q_blk=blk(0), k_blk=blk(1), v_blk=blk(2), groups=groups, n_tiles=n_tiles,
        slopes=_alibi_slopes(D_HEADS), sinks=jnp.zeros((D_HEADS,), F32), use_sink=False)


def _trunk(x, mod, g_mix_pre, g_mix_post, g_mlp_pre, g_mlp_post, mlp_w1, mlp_w2,
           a_wqkv, a_wo, a_sink, b_wqkv, b_wo, b_rel_bias, c_wqkv, c_wo, c_lambda, c_subln, d_wqkv, d_wo):
    depth = mod.shape[0]
    d = x.shape[-1]
    for i in range(depth):
        sh1, sc1, gt1, sh2, sc2, gt2 = [mod[i, :, None, k * d:(k + 1) * d] for k in range(6)]
        kind, occ = i % N_MIXERS, i // N_MIXERS
        if kind == 0:
            o, wo = _mixer_a(x, g_mix_pre[i], sc1, sh1, a_wqkv[occ], a_sink[occ]), a_wo[occ]
        elif kind == 1:
            o, wo = _mixer_b(x, g_mix_pre[i], sc1, sh1, b_wqkv[occ], b_rel_bias[occ]), b_wo[occ]
        elif kind == 2:
            o, wo = _mixer_c(x, g_mix_pre[i], sc1, sh1, c_wqkv[occ], c_lambda[occ], c_subln[occ],
                             0.8 - 0.6 * math.exp(-0.3 * i)), c_wo[occ]
        else:
            o, wo = _mixer_d(x, g_mix_pre[i], sc1, sh1, d_wqkv[occ]), d_wo[occ]
        x = _out_call(o, wo.astype(BF16), x, gt1, g_mix_post[i])
        x = _mlp_call(x, g_mlp_pre[i], sc2, sh2, mlp_w1[i].astype(BF16), mlp_w2[i].astype(BF16),
                      gt2, g_mlp_post[i])
    return x


def kernel(x_prompt, x_sample, c_prompt, c_sample, ada_w, ada_b, g_mix_pre, g_mix_post, g_mlp_pre, g_mlp_post, mlp_w1, mlp_w2, a_wqkv, a_wo, a_sink, b_wqkv, b_wo, b_rel_bias, c_wqkv, c_wo, c_lambda, c_subln, d_wqkv, d_wo):
    bp, bs = c_prompt.shape[0], c_sample.shape[0]
    c_all = jnp.concatenate([c_prompt, c_sample], axis=0)
    pad = (-c_all.shape[0]) % 8
    c_all = jnp.pad(c_all, ((0, pad), (0, 0)))
    mod = _adaln_call(c_all, ada_w, ada_b)
    params = (g_mix_pre, g_mix_post, g_mlp_pre, g_mlp_post, mlp_w1, mlp_w2, a_wqkv, a_wo, a_sink,
              b_wqkv, b_wo, b_rel_bias, c_wqkv, c_wo, c_lambda, c_subln, d_wqkv, d_wo)
    y_prompt = _trunk(x_prompt, mod[:, :bp], *params)
    y_sample = _trunk(x_sample, mod[:, bp:bp + bs], *params)
    return (y_prompt, y_sample)
```

```python
import functools
import math

import jax
import jax.numpy as jnp
from jax import lax
from jax.experimental import pallas as pl
from jax.experimental.pallas import tpu as pltpu

F32 = jnp.float32
BF16 = jnp.bfloat16

D_MODEL = 1024
HEAD_DIM = 64
LANES = 128
BF16_ROWS = 16
D_FF = 4 * D_MODEL
EPS = 1e-6
NEG = -1e30
LOG2E = math.log2(math.e)
Q_SCALE = HEAD_DIM ** -0.5 * LOG2E
N_MIXERS = 4

A_HEADS, A_KV_HEADS, A_WINDOW = 16, 4, 128
B_HEADS, GRID_W, WIN_R, WIN_C = 16, 64, 8, 16
C_HEADS = 8
D_GROUPS = ((128, 1), (512, 4), (2048, 16))
D_HEADS = 8

VMEM_LIMIT_BYTES = 56 * 1024 * 1024

_NT = (((1,), (1,)), ((), ()))


def _cparams(*sem):
    return pltpu.CompilerParams(dimension_semantics=sem, vmem_limit_bytes=VMEM_LIMIT_BYTES)


def _smem_spec():
    return pl.BlockSpec(memory_space=pltpu.SMEM)


def _alibi_slopes(n):
    return 2.0 ** (-8.0 * jnp.arange(1, n + 1, dtype=F32) / n)


def _modulated_norm(x, g, sc, sh):
    ms = jnp.mean(x * x, axis=-1, keepdims=True)
    return (x * lax.rsqrt(ms + EPS) * g) * (1.0 + sc) + sh


def _gated_norm_residual(x, h, g, gt):
    ms = jnp.mean(h * h, axis=-1, keepdims=True)
    return x + gt * (h * lax.rsqrt(ms + EPS) * g)


def _head_select(q, lane, hh):
    keep = (lane >= HEAD_DIM) if hh else (lane < HEAD_DIM)
    return jnp.where(keep, q, jnp.zeros_like(q))


def _scaled_q_weights(wq):
    return (wq * Q_SCALE).astype(BF16)


def _adaln_kernel(c_ref, w_ref, b_ref, o_ref):
    c = c_ref[...]
    ca = (c * jax.nn.sigmoid(c)).astype(BF16)
    o_ref[0] = jnp.dot(ca, w_ref[0].astype(BF16), preferred_element_type=F32) + b_ref[0]


def _adaln_call(c_pad, ada_w, ada_b):
    depth, d, n = ada_w.shape
    bp = c_pad.shape[0]
    tn = 1536
    return pl.pallas_call(
        _adaln_kernel,
        grid=(depth, n // tn),
        in_specs=[
            pl.BlockSpec((bp, d), lambda l, j: (0, 0)),
            pl.BlockSpec((1, d, tn), lambda l, j: (l, 0, j)),
            pl.BlockSpec((1, 1, tn), lambda l, j: (l, 0, j)),
        ],
        out_specs=pl.BlockSpec((1, bp, tn), lambda l, j: (l, 0, j)),
        out_shape=jax.ShapeDtypeStruct((depth, bp, n), F32),
        compiler_params=_cparams("parallel", "parallel"),
        name="adaln",
    )(c_pad, ada_w, ada_b.reshape(depth, 1, n))


def _qkv_kernel(*refs, n_chunk, has_t):
    if has_t:
        x_ref, g_ref, sc_ref, sh_ref, w_ref, wt_ref, o_ref, ot_ref = refs
    else:
        x_ref, g_ref, sc_ref, sh_ref, w_ref, o_ref = refs
    h = _modulated_norm(x_ref[0], g_ref[...], sc_ref[0], sh_ref[0]).astype(BF16)
    n = w_ref.shape[1]
    for c0 in range(0, n, n_chunk):
        o_ref[0, 0, :, c0:c0 + n_chunk] = jnp.dot(
            h, w_ref[:, c0:c0 + n_chunk], preferred_element_type=F32).astype(o_ref.dtype)
    if has_t:
        ht = lax.dot_general(wt_ref[...], h, _NT, preferred_element_type=F32).astype(ot_ref.dtype)
        ot_ref[0, :, 0] = ht.reshape(ot_ref.shape[1], LANES, ht.shape[1])


def _qkv_call(x, g, sc, sh, w, wt=None, dil=1, tm=256):
    b, s, d = x.shape
    n = w.shape[1]
    sd = s // dil
    tm = min(tm, sd)
    in_specs = [
        pl.BlockSpec((1, tm, d), lambda bi, r, i: (bi, i, r)),
        pl.BlockSpec((1, d), lambda bi, r, i: (0, 0)),
        pl.BlockSpec((1, 1, d), lambda bi, r, i: (bi, 0, 0)),
        pl.BlockSpec((1, 1, d), lambda bi, r, i: (bi, 0, 0)),
        pl.BlockSpec((d, n), lambda bi, r, i: (0, 0)),
    ]
    out_shape = [jax.ShapeDtypeStruct((b, dil, sd, n), BF16)]
    out_specs = [pl.BlockSpec((1, 1, tm, n), lambda bi, r, i: (bi, r, i, 0))]
    args = [x.reshape(b, sd, dil * d), g.reshape(1, d), sc, sh, w]
    if wt is not None:
        assert dil == 1
        nt = wt.shape[0]
        in_specs.append(pl.BlockSpec((nt, d), lambda bi, r, i: (0, 0)))
        out_shape.append(jax.ShapeDtypeStruct((b, nt // LANES, s // tm, LANES, tm), BF16))
        out_specs.append(pl.BlockSpec((1, nt // LANES, 1, LANES, tm), lambda bi, r, i: (bi, 0, i, 0, 0)))
        args.append(wt)
    outs = pl.pallas_call(
        functools.partial(_qkv_kernel, n_chunk=512, has_t=wt is not None),
        grid=(b, dil, sd // tm),
        in_specs=in_specs,
        out_specs=out_specs,
        out_shape=out_shape,
        compiler_params=_cparams("parallel", "parallel", "parallel"),
        name="qkv_proj",
    )(*args)
    qkv = outs[0].reshape(b * dil, sd, n)
    return (qkv, outs[1]) if wt is not None else qkv


def _out_kernel(*refs, n_mix):
    if n_mix:
        o_refs, lse_refs = refs[:n_mix], refs[n_mix:2 * n_mix]
        w_ref, x_ref, gt_ref, g_ref, y_ref = refs[2 * n_mix:]
        m = lse_refs[0][0]
        for r in lse_refs[1:]:
            m = jnp.maximum(m, r[0])
        num = jnp.zeros(m.shape, F32)
        den = jnp.zeros(m.shape, F32)
        for o_r, l_r in zip(o_refs, lse_refs):
            e = jnp.exp2(l_r[0] - m)
            den = den + e
            num = num + e * o_r[0]
        o = (num / den).astype(BF16)
    else:
        o_ref, w_ref, x_ref, gt_ref, g_ref, y_ref = refs
        o = o_ref[0]
    h = jnp.dot(o, w_ref[...], preferred_element_type=F32)
    y_ref[0] = _gated_norm_residual(x_ref[0], h, g_ref[...], gt_ref[0])


def _out_call(o_list, lse_list, w, x, gt, g):
    b, s, d = x.shape
    kd = w.shape[0]
    tm = min(512, s)
    n_mix = len(lse_list)
    tile = pl.BlockSpec((1, tm, kd), lambda bi, i: (bi, i, 0))
    return pl.pallas_call(
        functools.partial(_out_kernel, n_mix=n_mix),
        grid=(b, s // tm),
        in_specs=[tile] * (len(o_list) + n_mix) + [
            pl.BlockSpec((kd, d), lambda bi, i: (0, 0)),
            pl.BlockSpec((1, tm, d), lambda bi, i: (bi, i, 0)),
            pl.BlockSpec((1, 1, d), lambda bi, i: (bi, 0, 0)),
            pl.BlockSpec((1, d), lambda bi, i: (0, 0)),
        ],
        out_specs=pl.BlockSpec((1, tm, d), lambda bi, i: (bi, i, 0)),
        out_shape=jax.ShapeDtypeStruct((b, s, d), F32),
        compiler_params=_cparams("parallel", "parallel"),
        name="out_proj",
    )(*o_list, *lse_list, w, x, gt, g.reshape(1, d))


def _mlp_kernel(x_ref, g1_ref, sc_ref, sh_ref, w1_ref, w2_ref, gt_ref, g2_ref, y_ref, *, f_chunk):
    x = x_ref[0]
    h = _modulated_norm(x, g1_ref[...], sc_ref[0], sh_ref[0]).astype(BF16)
    acc = jnp.zeros(x.shape, F32)
    for c0 in range(0, w1_ref.shape[1], f_chunk):
        a = jnp.dot(h, w1_ref[:, c0:c0 + f_chunk], preferred_element_type=F32)
        a = jnp.maximum(a, 0.0)
        a = (a * a).astype(BF16)
        acc = acc + jnp.dot(a, w2_ref[c0:c0 + f_chunk, :], preferred_element_type=F32)
    y_ref[0] = _gated_norm_residual(x, acc, g2_ref[...], gt_ref[0])


def _mlp_call(x, g1, sc, sh, w1, w2, gt, g2):
    b, s, d = x.shape
    f = w1.shape[1]
    tm = min(512, s)
    vec = pl.BlockSpec((1, 1, d), lambda bi, i: (bi, 0, 0))
    gain = pl.BlockSpec((1, d), lambda bi, i: (0, 0))
    return pl.pallas_call(
        functools.partial(_mlp_kernel, f_chunk=1024),
        grid=(b, s // tm),
        in_specs=[
            pl.BlockSpec((1, tm, d), lambda bi, i: (bi, i, 0)),
            gain, vec, vec,
            pl.BlockSpec((d, f), lambda bi, i: (0, 0), pipeline_mode=pl.Buffered(1)),
            pl.BlockSpec((f, d), lambda bi, i: (0, 0), pipeline_mode=pl.Buffered(1)),
            vec, gain,
        ],
        out_specs=pl.BlockSpec((1, tm, d), lambda bi, i: (bi, i, 0)),
        out_shape=jax.ShapeDtypeStruct((b, s, d), F32),
        compiler_params=_cparams("parallel", "parallel"),
        name="mlp",
    )(x, g1.reshape(1, d), sc, sh, w1, w2, gt, g2.reshape(1, d))


def _banded_kernel(slopes_ref, sink_ref, q_ref, k_ref, v_ref, *out_refs,
                   half, tq_sub, n_sub, seq, use_sink, emit_lse):
    o_ref = out_refs[0]
    p = pl.program_id(1)
    t0 = pl.program_id(2) * (tq_sub * n_sub)
    span = tq_sub + 2 * half
    lane = lax.broadcasted_iota(jnp.int32, (tq_sub, LANES), 1)
    base = (lax.broadcasted_iota(jnp.int32, (tq_sub, span), 1)
            - lax.broadcasted_iota(jnp.int32, (tq_sub, span), 0) - half)
    for sb in range(n_sub):
        q0 = t0 + sb * tq_sub
        start = pl.multiple_of(jnp.clip(q0 - half, 0, seq - span), HEAD_DIM)
        kw = k_ref[0, pl.ds(start, span), :]
        vw = v_ref[0, pl.ds(start, span), :]
        dist = jnp.abs(base + (start - q0 + half))
        valid = dist <= half
        distf = dist.astype(F32)
        q = q_ref[0, sb * tq_sub:(sb + 1) * tq_sub, :]
        res, lses = [], []
        for hh in range(2):
            slope = slopes_ref[2 * p + hh]
            s = lax.dot_general(_head_select(q, lane, hh), kw, _NT, preferred_element_type=F32)
            s = jnp.where(valid, s - slope * distf, NEG)
            m = jnp.max(s, axis=-1, keepdims=True)
            if use_sink:
                sink = sink_ref[2 * p + hh]
                m = jnp.maximum(m, sink)
            pr = jnp.exp2(s - m)
            den = jnp.sum(pr, axis=-1, keepdims=True)
            if use_sink:
                den = den + jnp.exp2(sink - m)
            acc = jnp.dot(pr.astype(BF16), vw, preferred_element_type=F32)
            res.append(acc / den)
            lses.append(m + jnp.log2(den))
        rows = slice(sb * tq_sub, (sb + 1) * tq_sub)
        o_ref[0, rows, :] = jnp.where(lane < HEAD_DIM, res[0], res[1]).astype(o_ref.dtype)
        if emit_lse:
            out_refs[1][0, rows, :] = jnp.where(lane < HEAD_DIM, lses[0], lses[1])


def _banded_call(qkv, q_blk, k_blk, v_blk, half, dil, n_tiles, slopes, sinks, use_sink, emit_lse):
    bd, sd, _ = qkv.shape
    b = bd // dil
    tq_sub = 128
    n_sub = min(4, sd // tq_sub)
    tq = tq_sub * n_sub
    out_dtype = F32 if emit_lse else BF16
    out_spec = pl.BlockSpec((1, tq, LANES), lambda bb, p, i: (bb // dil, i, (bb % dil) * n_tiles + p))
    out_shape = [jax.ShapeDtypeStruct((b, sd, dil * n_tiles * LANES), out_dtype)]
    out_specs = [out_spec]
    if emit_lse:
        out_shape.append(jax.ShapeDtypeStruct((b, sd, dil * n_tiles * LANES), F32))
        out_specs.append(out_spec)
    outs = pl.pallas_call(
        functools.partial(_banded_kernel, half=half, tq_sub=tq_sub, n_sub=n_sub, seq=sd,
                          use_sink=use_sink, emit_lse=emit_lse),
        grid=(bd, n_tiles, sd // tq),
        in_specs=[
            _smem_spec(), _smem_spec(),
            pl.BlockSpec((1, tq, LANES), lambda bb, p, i: (bb, i, q_blk(p))),
            pl.BlockSpec((1, sd, LANES), lambda bb, p, i: (bb, 0, k_blk(p))),
            pl.BlockSpec((1, sd, LANES), lambda bb, p, i: (bb, 0, v_blk(p))),
        ],
        out_specs=out_specs,
        out_shape=out_shape,
        compiler_params=_cparams("parallel", "parallel", "arbitrary"),
        name="banded_attn",
    )(slopes, sinks, qkv, qkv, qkv)
    return [o.reshape(b, sd * dil, n_tiles * LANES) for o in outs]


def _nbr_kernel(q_ref, k_ref, v_ref, t_ref, o_ref, *, rows_per_tile, n_rows):
    i = pl.program_id(2)
    nkeys = WIN_R * GRID_W
    lane = lax.broadcasted_iota(jnp.int32, (GRID_W, LANES), 1)
    for rr in range(rows_per_tile):
        r = i * rows_per_tile + rr
        rs = jnp.clip(r - WIN_R // 2, 0, n_rows - WIN_R)
        off = rs - r + (WIN_R - 1)
        start = pl.multiple_of(rs * GRID_W, GRID_W)
        kw = k_ref[0, pl.ds(start, nkeys), :]
        vw = v_ref[0, pl.ds(start, nkeys), :]
        q = q_ref[0, rr * GRID_W:(rr + 1) * GRID_W, :]
        res = []
        for hh in range(2):
            s = lax.dot_general(_head_select(q, lane, hh), kw, _NT, preferred_element_type=F32)
            s = s + t_ref[0, hh, off]
            m = jnp.max(s, axis=-1, keepdims=True)
            pr = jnp.exp2(s - m)
            den = jnp.sum(pr, axis=-1, keepdims=True)
            res.append(jnp.dot(pr.astype(BF16), vw, preferred_element_type=F32) / den)
        o_ref[0, rr * GRID_W:(rr + 1) * GRID_W, :] = jnp.where(
            lane < HEAD_DIM, res[0], res[1]).astype(o_ref.dtype)


def _nbr_bias_table(rel_bias):
    col = jnp.arange(GRID_W)
    cs = jnp.clip(col - WIN_C // 2, 0, GRID_W - WIN_C)
    col_valid = (col[None, :] >= cs[:, None]) & (col[None, :] < cs[:, None] + WIN_C)
    dc_idx = jnp.clip(col[None, :] - col[:, None] + WIN_C - 1, 0, 2 * WIN_C - 2)
    dr = jnp.arange(WIN_R)[:, None] + jnp.arange(WIN_R)[None, :]
    tab = rel_bias.astype(F32)[:, dr] * LOG2E
    tab = tab[..., dc_idx]
    tab = jnp.where(col_valid, tab, NEG)
    tab = jnp.transpose(tab, (0, 1, 3, 2, 4))
    h = tab.shape[0]
    return tab.reshape(h // 2, 2, WIN_R, GRID_W, WIN_R * GRID_W)


def _nbr_call(qkv, table):
    b, s, _ = qkv.shape
    n_tiles = B_HEADS // 2
    n_rows = s // GRID_W
    rows_per_tile = min(8, n_rows)
    tq = rows_per_tile * GRID_W
    kv = lambda off: pl.BlockSpec((1, s, LANES), lambda bi, p, i: (bi, 0, off + p))
    return pl.pallas_call(
        functools.partial(_nbr_kernel, rows_per_tile=rows_per_tile, n_rows=n_rows),
        grid=(b, n_tiles, s // tq),
        in_specs=[
            pl.BlockSpec((1, tq, LANES), lambda bi, p, i: (bi, i, p)),
            kv(n_tiles), kv(2 * n_tiles),
            pl.BlockSpec((1, 2, WIN_R, GRID_W, WIN_R * GRID_W), lambda bi, p, i: (p, 0, 0, 0, 0)),
        ],
        out_specs=pl.BlockSpec((1, tq, LANES), lambda bi, p, i: (bi, i, p)),
        out_shape=jax.ShapeDtypeStruct((b, s, n_tiles * LANES), BF16),
        compiler_params=_cparams("parallel", "parallel", "arbitrary"),
        name="nbr_attn",
    )(qkv, qkv, qkv, table)


_POS_SPLIT = 32


def _diff_kernel(slopes_ref, lam_ref, q_ref, k_ref, vt_ref, kp_ref, g_ref, o_ref,
                 qa_s, m_s, acc_s, s_s, p_s, al_s, rb_s, mx_s, *, t, n_tiles, post_scale):
    h = pl.program_id(1)
    i = pl.program_id(2)
    slope = slopes_ref[h]
    n_off = n_tiles - 1
    ones_rows = jnp.ones((BF16_ROWS, t), BF16)

    q = q_ref[0]
    lane = lax.broadcasted_iota(jnp.int32, (t, LANES), 1)
    c = jnp.full((t, LANES), -slope, F32)
    c_hi = c.astype(BF16).astype(F32)
    aug = jnp.where(lane < 2, c_hi, jnp.where(lane < 4, c - c_hi, 0.0)).astype(BF16)
    for mp in range(2):
        qa_s[mp, :, :LANES] = _head_select(q, lane, mp)
        qa_s[mp, :, LANES:] = aug

    kd = k_ref[0, pl.ds(pl.multiple_of(i * t, t), t), :]
    kpos = lax.broadcasted_iota(jnp.int32, (t, t), 0)
    qpos = lax.broadcasted_iota(jnp.int32, (t, t), 1)
    bias = slope * jnp.abs(kpos - qpos).astype(F32)
    vd = jnp.concatenate([vt_ref[0, 0, i], ones_rows], axis=0)
    for mp in range(2):
        s = lax.dot_general(kd, qa_s[mp, :, :LANES], _NT, preferred_element_type=F32) - bias
        m = jnp.max(s, axis=0, keepdims=True)
        acc_s[mp] = jnp.dot(vd, jnp.exp2(s - m).astype(BF16), preferred_element_type=F32)
        m_s[mp] = m

    def tile_of(n):
        after = n >= i
        return n + jnp.where(after, 1, 0), after

    def stage_scores(n, slot):
        j, after = tile_of(n)
        kt = k_ref[0, pl.ds(pl.multiple_of(j * t, t), t), :]
        ka = jnp.concatenate([kt, kp_ref[jnp.where(after, 0, 1)]], axis=1)
        sgn = jnp.where(after, 1.0, -1.0).astype(F32)
        qrow = lax.broadcasted_iota(jnp.int32, (1, t), 1).astype(F32)
        blk = (jnp.abs(j - i) * t).astype(F32)
        rb_s[slot] = slope * (sgn * qrow - blk)
        for mp in range(2):
            s = lax.dot_general(ka, qa_s[mp], _NT, preferred_element_type=F32)
            s_s[slot, mp] = s
            mx_s[slot, mp] = jnp.max(s, axis=0, keepdims=True)

    def stage_softmax(slot):
        rb = rb_s[slot]
        for mp in range(2):
            m_old = m_s[mp]
            m_new = jnp.maximum(m_old, mx_s[slot, mp] + rb)
            al_s[slot, mp] = jnp.exp2(m_old - m_new)
            p_s[slot, mp] = jnp.exp2(s_s[slot, mp] - (m_new - rb)).astype(BF16)
            m_s[mp] = m_new

    def stage_values(n, slot):
        j, _ = tile_of(n)
        va = jnp.concatenate([vt_ref[0, 0, j], ones_rows], axis=0)
        for mp in range(2):
            acc_s[mp] = al_s[slot, mp] * acc_s[mp] + jnp.dot(va, p_s[slot, mp], preferred_element_type=F32)

    def step(n, par, do_scores, do_softmax, do_values):
        if do_values:
            stage_values(n - 2, par)
        if do_softmax:
            stage_softmax(1 - par)
        if do_scores:
            stage_scores(n, par)

    def static_step(n):
        step(n, n % 2, n < n_off, 1 <= n <= n_off, 2 <= n <= n_off + 1)

    for n in range(2):
        static_step(n)
    n_pairs = max(0, n_off - 2) // 2
    if n_pairs:
        def pair(pi, carry):
            n = 2 + 2 * pi
            step(n, 0, True, True, True)
            step(n + 1, 1, True, True, True)
            return carry
        lax.fori_loop(0, n_pairs, pair, 0)
    for n in range(2 + 2 * n_pairs, n_off + 2):
        static_step(n)

    o0 = acc_s[0, :LANES] / acc_s[0, LANES:LANES + 1]
    o1 = acc_s[1, :LANES] / acc_s[1, LANES:LANES + 1]
    o = o0 - lam_ref[0] * o1
    ms = jnp.mean(o * o, axis=0, keepdims=True)
    y = (o * lax.rsqrt(ms + EPS) * g_ref[...]) * post_scale
    o_ref[0] = y.T.astype(o_ref.dtype)


def _key_offset_tiles(t):
    r = jnp.arange(t)
    hi = (r // _POS_SPLIT * _POS_SPLIT).astype(F32)
    lo = (r % _POS_SPLIT).astype(F32)
    tile = jnp.zeros((t, LANES), F32)
    tile = tile.at[:, 0].set(hi).at[:, 1].set(lo).at[:, 2].set(hi).at[:, 3].set(lo)
    return jnp.stack([tile, -tile]).astype(BF16)


def _diff_tile(s):
    return min(512, s)


def _diff_call(qk, vt, slopes, lam, subln, post_scale):
    b, s, _ = qk.shape
    t = _diff_tile(s)
    n_tiles = s // t
    assert t <= _POS_SPLIT * BF16_ROWS
    return pl.pallas_call(
        functools.partial(_diff_kernel, t=t, n_tiles=n_tiles, post_scale=post_scale),
        grid=(b, C_HEADS, n_tiles),
        in_specs=[
            _smem_spec(), _smem_spec(),
            pl.BlockSpec((1, t, LANES), lambda bi, h, i: (bi, i, h)),
            pl.BlockSpec((1, s, LANES), lambda bi, h, i: (bi, 0, C_HEADS + h)),
            pl.BlockSpec((1, 1, n_tiles, LANES, t), lambda bi, h, i: (bi, h, 0, 0, 0)),
            pl.BlockSpec((2, t, LANES), lambda bi, h, i: (0, 0, 0)),
            pl.BlockSpec((LANES, 1), lambda bi, h, i: (0, 0)),
        ],
        out_specs=pl.BlockSpec((1, t, LANES), lambda bi, h, i: (bi, i, h)),
        out_shape=jax.ShapeDtypeStruct((b, s, C_HEADS * LANES), BF16),
        scratch_shapes=[
            pltpu.VMEM((2, t, 2 * LANES), BF16),
            pltpu.VMEM((2, 1, t), F32),
            pltpu.VMEM((2, LANES + BF16_ROWS, t), F32),
            pltpu.VMEM((2, 2, t, t), F32),
            pltpu.VMEM((2, 2, t, t), BF16),
            pltpu.VMEM((2, 2, 1, t), F32),
            pltpu.VMEM((2, 1, t), F32),
            pltpu.VMEM((2, 2, 1, t), F32),
        ],
        compiler_params=_cparams("parallel", "parallel", "parallel"),
        name="diff_attn",
    )(slopes, lam, qk, qk, vt, _key_offset_tiles(t), subln.astype(F32).reshape(LANES, 1))


def _mixer_a(x, g, sc, sh, wqkv, sink):
    d = x.shape[-1]
    nq = A_HEADS * HEAD_DIM
    nkv = A_KV_HEADS * HEAD_DIM

    def dup(wpart):
        wpart = wpart.reshape(d, A_KV_HEADS, 1, HEAD_DIM)
        return jnp.broadcast_to(wpart, (d, A_KV_HEADS, 2, HEAD_DIM)).reshape(d, 2 * nkv).astype(BF16)

    w = jnp.concatenate([_scaled_q_weights(wqkv[:, :nq]), dup(wqkv[:, nq:nq + nkv]),
                         dup(wqkv[:, nq + nkv:])], axis=1)
    qkv = _qkv_call(x, g, sc, sh, w)
    q_tiles = nq // LANES
    kv_tiles = 2 * nkv // LANES
    pairs_per_kv = A_HEADS // A_KV_HEADS // 2
    return _banded_call(
        qkv, q_blk=lambda p: p, k_blk=lambda p: q_tiles + p // pairs_per_kv,
        v_blk=lambda p: q_tiles + kv_tiles + p // pairs_per_kv,
        half=A_WINDOW, dil=1, n_tiles=A_HEADS // 2, slopes=_alibi_slopes(A_HEADS) * LOG2E,
        sinks=sink.astype(F32) * LOG2E, use_sink=True, emit_lse=False), []


def _mixer_b(x, g, sc, sh, wqkv, rel_bias):
    nq = B_HEADS * HEAD_DIM
    w = jnp.concatenate([_scaled_q_weights(wqkv[:, :nq]), wqkv[:, nq:].astype(BF16)], axis=1)
    qkv = _qkv_call(x, g, sc, sh, w)
    return [_nbr_call(qkv, _nbr_bias_table(rel_bias))], []


def _mixer_c(x, g, sc, sh, wqkv, lam, subln, lam_init):
    nq = 2 * C_HEADS * HEAD_DIM
    w = jnp.concatenate([_scaled_q_weights(wqkv[:, :nq]), wqkv[:, nq:2 * nq].astype(BF16)], axis=1)
    qk, vt = _qkv_call(x, g, sc, sh, w, wqkv[:, 2 * nq:].T.astype(BF16), tm=_diff_tile(x.shape[1]))
    lf = lam.astype(F32)
    lam_full = jnp.exp(jnp.sum(lf[0] * lf[1])) - jnp.exp(jnp.sum(lf[2] * lf[3])) + lam_init
    return [_diff_call(qk, vt, _alibi_slopes(C_HEADS) * LOG2E, lam_full.reshape(1), subln,
                       1.0 - lam_init)], []


def _mixer_d(x, g, sc, sh, wqkv):
    d = x.shape[-1]
    nh = D_HEADS * HEAD_DIM
    n_tiles = D_HEADS // 2
    outs, lses = [], []
    for gi, (win, dil) in enumerate(D_GROUPS):
        wg = wqkv[:, gi * 3 * nh:(gi + 1) * 3 * nh]
        w = jnp.concatenate([_scaled_q_weights(wg[:, :nh]), wg[:, nh:].astype(BF16)], axis=1)
        qkv = _qkv_call(x, g, sc, sh, w, dil=dil)
        o, lse = _banded_call(
            qkv, q_blk=lambda p: p, k_blk=lambda p: n_tiles + p, v_blk=lambda p: 2 * n_tiles + p,
            half=win // (2 * dil), dil=dil, n_tiles=n_tiles,
            slopes=_alibi_slopes(D_HEADS) * (LOG2E * dil), sinks=jnp.zeros((D_HEADS,), F32),
            use_sink=False, emit_lse=True)
        outs.append(o)
        lses.append(lse)
    return outs, lses


def _trunk(x, mod, g_mix_pre, g_mix_post, g_mlp_pre, g_mlp_post, mlp_w1, mlp_w2,
           a_wqkv, a_wo, a_sink, b_wqkv, b_wo, b_rel_bias, c_wqkv, c_wo, c_lambda, c_subln, d_wqkv, d_wo):
    depth = mod.shape[0]
    d = x.shape[-1]
    for i in range(depth):
        sh1, sc1, gt1, sh2, sc2, gt2 = [mod[i, :, None, k * d:(k + 1) * d] for k in range(6)]
        kind, occ = i % N_MIXERS, i // N_MIXERS
        if kind == 0:
            (o, lse), wo = _mixer_a(x, g_mix_pre[i], sc1, sh1, a_wqkv[occ], a_sink[occ]), a_wo[occ]
        elif kind == 1:
            (o, lse), wo = _mixer_b(x, g_mix_pre[i], sc1, sh1, b_wqkv[occ], b_rel_bias[occ]), b_wo[occ]
        elif kind == 2:
            (o, lse), wo = _mixer_c(x, g_mix_pre[i], sc1, sh1, c_wqkv[occ], c_lambda[occ], c_subln[occ],
                                    0.8 - 0.6 * math.exp(-0.3 * i)), c_wo[occ]
        else:
            (o, lse), wo = _mixer_d(x, g_mix_pre[i], sc1, sh1, d_wqkv[occ]), d_wo[occ]
        x = _out_call(o, lse, wo.astype(BF16), x, gt1, g_mix_post[i])
        x = _mlp_call(x, g_mlp_pre[i], sc2, sh2, mlp_w1[i].astype(BF16), mlp_w2[i].astype(BF16),
                      gt2, g_mlp_post[i])
    return x


def kernel(x_prompt, x_sample, c_prompt, c_sample, ada_w, ada_b, g_mix_pre, g_mix_post, g_mlp_pre, g_mlp_post, mlp_w1, mlp_w2, a_wqkv, a_wo, a_sink, b_wqkv, b_wo, b_rel_bias, c_wqkv, c_wo, c_lambda, c_subln, d_wqkv, d_wo):
    bp, bs = c_prompt.shape[0], c_sample.shape[0]
    c_all = jnp.concatenate([c_prompt, c_sample], axis=0)
    pad = (-c_all.shape[0]) % 8
    c_all = jnp.pad(c_all, ((0, pad), (0, 0)))
    mod = _adaln_call(c_all, ada_w, ada_b)
    params = (g_mix_pre, g_mix_post, g_mlp_pre, g_mlp_post, mlp_w1, mlp_w2, a_wqkv, a_wo, a_sink,
              b_wqkv, b_wo, b_rel_bias, c_wqkv, c_wo, c_lambda, c_subln, d_wqkv, d_wo)
    y_prompt = _trunk(x_prompt, mod[:, :bp], *params)
    y_sample = _trunk(x_sample, mod[:, bp:bp + bs], *params)
    return (y_prompt, y_sample)
```

```python
import functools
import math

import jax
import jax.numpy as jnp
from jax import lax
from jax.experimental import pallas as pl
from jax.experimental.pallas import tpu as pltpu

F32 = jnp.float32
BF16 = jnp.bfloat16

D_MODEL = 1024
HEAD_DIM = 64
LANES = 128
BF16_ROWS = 16
D_FF = 4 * D_MODEL
EPS = 1e-6
NEG = -1e30
LOG2E = math.log2(math.e)
Q_SCALE = HEAD_DIM ** -0.5 * LOG2E
N_MIXERS = 4

A_HEADS, A_KV_HEADS, A_WINDOW = 16, 4, 128
B_HEADS, GRID_W, WIN_R, WIN_C = 16, 64, 8, 16
C_HEADS = 8
D_GROUPS = ((128, 1), (512, 4), (2048, 16))
D_HEADS = 8

VMEM_LIMIT_BYTES = 56 * 1024 * 1024

_NT = (((1,), (1,)), ((), ()))


def _cparams(*sem):
    return pltpu.CompilerParams(dimension_semantics=sem, vmem_limit_bytes=VMEM_LIMIT_BYTES)


def _smem_spec():
    return pl.BlockSpec(memory_space=pltpu.SMEM)


def _alibi_slopes(n):
    return 2.0 ** (-8.0 * jnp.arange(1, n + 1, dtype=F32) / n)


def _modulated_norm(x, g, sc, sh):
    ms = jnp.mean(x * x, axis=-1, keepdims=True)
    return (x * lax.rsqrt(ms + EPS) * g) * (1.0 + sc) + sh


def _gated_norm_residual(x, h, g, gt):
    ms = jnp.mean(h * h, axis=-1, keepdims=True)
    return x + gt * (h * lax.rsqrt(ms + EPS) * g)


def _head_select(q, lane, hh):
    keep = (lane >= HEAD_DIM) if hh else (lane < HEAD_DIM)
    return jnp.where(keep, q, jnp.zeros_like(q))


def _scaled_q_weights(wq):
    return (wq * Q_SCALE).astype(BF16)


def _adaln_kernel(c_ref, w_ref, b_ref, o_ref):
    c = c_ref[...]
    ca = (c * jax.nn.sigmoid(c)).astype(BF16)
    o_ref[0] = jnp.dot(ca, w_ref[0].astype(BF16), preferred_element_type=F32) + b_ref[0]


def _adaln_call(c_pad, ada_w, ada_b):
    depth, d, n = ada_w.shape
    bp = c_pad.shape[0]
    tn = 1536
    return pl.pallas_call(
        _adaln_kernel,
        grid=(depth, n // tn),
        in_specs=[
            pl.BlockSpec((bp, d), lambda l, j: (0, 0)),
            pl.BlockSpec((1, d, tn), lambda l, j: (l, 0, j)),
            pl.BlockSpec((1, 1, tn), lambda l, j: (l, 0, j)),
        ],
        out_specs=pl.BlockSpec((1, bp, tn), lambda l, j: (l, 0, j)),
        out_shape=jax.ShapeDtypeStruct((depth, bp, n), F32),
        compiler_params=_cparams("parallel", "parallel"),
        name="adaln",
    )(c_pad, ada_w, ada_b.reshape(depth, 1, n))


def _qkv_kernel(*refs, n_chunk, has_t, dil):
    if has_t:
        x_ref, g_ref, sc_ref, sh_ref, w_ref, wt_ref, o_ref, ot_ref = refs
    elif dil > 1:
        x_ref, g_ref, sc_ref, sh_ref, w_ref, o_ref, slab_s, h_s = refs
    else:
        x_ref, g_ref, sc_ref, sh_ref, w_ref, o_ref = refs
    tm = o_ref.shape[2]
    hn = _modulated_norm(x_ref[0], g_ref[...], sc_ref[0], sh_ref[0])
    if dil == 1:
        h = hn.astype(BF16)
    else:
        for c in range(slab_s.shape[0]):
            slab_s[c] = hn[:, c * LANES:(c + 1) * LANES]
        for rho in range(dil):
            for c in range(slab_s.shape[0]):
                h_s[rho * tm:(rho + 1) * tm, c * LANES:(c + 1) * LANES] = (
                    slab_s[c, pl.ds(rho, tm, stride=dil), :].astype(BF16))
        h = h_s[...]
    n = w_ref.shape[1]
    for c0 in range(0, n, n_chunk):
        oc = jnp.dot(h, w_ref[:, c0:c0 + n_chunk], preferred_element_type=F32).astype(o_ref.dtype)
        o_ref[0, :, :, c0:c0 + n_chunk] = oc.reshape(dil, tm, n_chunk)
    if has_t:
        ht = lax.dot_general(wt_ref[...], h, _NT, preferred_element_type=F32).astype(ot_ref.dtype)
        ot_ref[0, :, 0] = ht.reshape(ot_ref.shape[1], LANES, ht.shape[1])


def _qkv_call(x, g, sc, sh, w, wt=None, dil=1, tile=256):
    b, s, d = x.shape
    n = w.shape[1]
    sd = s // dil
    tile = min(tile, s)
    tm = tile // dil
    assert tm % BF16_ROWS == 0
    in_specs = [
        pl.BlockSpec((1, tile, d), lambda bi, i: (bi, i, 0)),
        pl.BlockSpec((1, d), lambda bi, i: (0, 0)),
        pl.BlockSpec((1, 1, d), lambda bi, i: (bi, 0, 0)),
        pl.BlockSpec((1, 1, d), lambda bi, i: (bi, 0, 0)),
        pl.BlockSpec((d, n), lambda bi, i: (0, 0)),
    ]
    out_shape = [jax.ShapeDtypeStruct((b, dil, sd, n), BF16)]
    out_specs = [pl.BlockSpec((1, dil, tm, n), lambda bi, i: (bi, 0, i, 0))]
    args = [x, g.reshape(1, d), sc, sh, w]
    scratch = []
    if wt is not None:
        assert dil == 1
        nt = wt.shape[0]
        in_specs.append(pl.BlockSpec((nt, d), lambda bi, i: (0, 0)))
        out_shape.append(jax.ShapeDtypeStruct((b, nt // LANES, s // tile, LANES, tile), BF16))
        out_specs.append(pl.BlockSpec((1, nt // LANES, 1, LANES, tile), lambda bi, i: (bi, 0, i, 0, 0)))
        args.append(wt)
    if dil > 1:
        scratch += [pltpu.VMEM((d // LANES, tile, LANES), F32), pltpu.VMEM((tile, d), BF16)]
    outs = pl.pallas_call(
        functools.partial(_qkv_kernel, n_chunk=512, has_t=wt is not None, dil=dil),
        grid=(b, s // tile),
        in_specs=in_specs,
        out_specs=out_specs,
        out_shape=out_shape,
        scratch_shapes=scratch,
        compiler_params=_cparams("parallel", "parallel"),
        name="qkv_proj",
    )(*args)
    return (outs[0], outs[1]) if wt is not None else outs[0]


def _out_kernel(*refs, n_mix):
    if n_mix:
        o_refs, lse_refs = refs[:n_mix], refs[n_mix:2 * n_mix]
        w_ref, x_ref, gt_ref, g_ref, y_ref = refs[2 * n_mix:]
        m = lse_refs[0][0]
        for r in lse_refs[1:]:
            m = jnp.maximum(m, r[0])
        num = jnp.zeros(m.shape, F32)
        den = jnp.zeros(m.shape, F32)
        for o_r, l_r in zip(o_refs, lse_refs):
            e = jnp.exp2(l_r[0] - m)
            den = den + e
            num = num + e * o_r[0]
        o = (num / den).astype(BF16)
    else:
        o_ref, w_ref, x_ref, gt_ref, g_ref, y_ref = refs
        o = o_ref[0]
    h = jnp.dot(o, w_ref[...], preferred_element_type=F32)
    y_ref[0] = _gated_norm_residual(x_ref[0], h, g_ref[...], gt_ref[0])


def _out_call(o_list, lse_list, w, x, gt, g):
    b, s, d = x.shape
    kd = w.shape[0]
    tm = min(512, s)
    n_mix = len(lse_list)
    tile = pl.BlockSpec((1, tm, kd), lambda bi, i: (bi, i, 0))
    return pl.pallas_call(
        functools.partial(_out_kernel, n_mix=n_mix),
        grid=(b, s // tm),
        in_specs=[tile] * (len(o_list) + n_mix) + [
            pl.BlockSpec((kd, d), lambda bi, i: (0, 0)),
            pl.BlockSpec((1, tm, d), lambda bi, i: (bi, i, 0)),
            pl.BlockSpec((1, 1, d), lambda bi, i: (bi, 0, 0)),
            pl.BlockSpec((1, d), lambda bi, i: (0, 0)),
        ],
        out_specs=pl.BlockSpec((1, tm, d), lambda bi, i: (bi, i, 0)),
        out_shape=jax.ShapeDtypeStruct((b, s, d), F32),
        compiler_params=_cparams("parallel", "parallel"),
        name="out_proj",
    )(*o_list, *lse_list, w, x, gt, g.reshape(1, d))


def _mlp_kernel(x_ref, g1_ref, sc_ref, sh_ref, w1_ref, w2_ref, gt_ref, g2_ref, y_ref, *, f_chunk):
    x = x_ref[0]
    h = _modulated_norm(x, g1_ref[...], sc_ref[0], sh_ref[0]).astype(BF16)
    acc = jnp.zeros(x.shape, F32)
    for c0 in range(0, w1_ref.shape[1], f_chunk):
        a = jnp.dot(h, w1_ref[:, c0:c0 + f_chunk], preferred_element_type=F32)
        a = jnp.maximum(a, 0.0)
        a = (a * a).astype(BF16)
        acc = acc + jnp.dot(a, w2_ref[c0:c0 + f_chunk, :], preferred_element_type=F32)
    y_ref[0] = _gated_norm_residual(x, acc, g2_ref[...], gt_ref[0])


def _mlp_call(x, g1, sc, sh, w1, w2, gt, g2):
    b, s, d = x.shape
    f = w1.shape[1]
    tm = min(512, s)
    vec = pl.BlockSpec((1, 1, d), lambda bi, i: (bi, 0, 0))
    gain = pl.BlockSpec((1, d), lambda bi, i: (0, 0))
    return pl.pallas_call(
        functools.partial(_mlp_kernel, f_chunk=1024),
        grid=(b, s // tm),
        in_specs=[
            pl.BlockSpec((1, tm, d), lambda bi, i: (bi, i, 0)),
            gain, vec, vec,
            pl.BlockSpec((d, f), lambda bi, i: (0, 0), pipeline_mode=pl.Buffered(1)),
            pl.BlockSpec((f, d), lambda bi, i: (0, 0), pipeline_mode=pl.Buffered(1)),
            vec, gain,
        ],
        out_specs=pl.BlockSpec((1, tm, d), lambda bi, i: (bi, i, 0)),
        out_shape=jax.ShapeDtypeStruct((b, s, d), F32),
        compiler_params=_cparams("parallel", "parallel"),
        name="mlp",
    )(x, g1.reshape(1, d), sc, sh, w1, w2, gt, g2.reshape(1, d))


def _banded_kernel(slopes_ref, sink_ref, q_ref, k_ref, v_ref, *out_refs,
                   half, tq_sub, n_sub, n_u, dil, seq, use_sink, emit_lse):
    o_ref = out_refs[0]
    n_heads = 2 * n_u
    head0 = pl.program_id(1) * n_heads
    span = tq_sub + 2 * half
    lane = lax.broadcasted_iota(jnp.int32, (tq_sub, LANES), 1)
    base = (lax.broadcasted_iota(jnp.int32, (tq_sub, span), 1)
            - lax.broadcasted_iota(jnp.int32, (tq_sub, span), 0) - half)
    for sb in range(n_sub):
        if dil == 1:
            r, q0 = 0, (pl.program_id(2) * n_sub + sb) * tq_sub
            qrows = slice(sb * tq_sub, (sb + 1) * tq_sub)
            rows = qrows
        else:
            r, q0 = sb, pl.program_id(2) * tq_sub
            qrows = slice(0, tq_sub)
            rows = pl.ds(r, tq_sub, stride=dil)
        start = pl.multiple_of(jnp.clip(q0 - half, 0, seq - span), HEAD_DIM)
        kw = k_ref[0, r, pl.ds(start, span), :]
        vw = v_ref[0, r, pl.ds(start, span), :]
        dist = jnp.abs(base + (start - q0 + half))
        valid = dist <= half
        distf = dist.astype(F32)
        qs = []
        for u in range(n_u):
            qu = q_ref[0, r, qrows, u * LANES:(u + 1) * LANES]
            qs += [_head_select(qu, lane, 0), _head_select(qu, lane, 1)]
        s_all = lax.dot_general(jnp.concatenate(qs, axis=0), kw, _NT, preferred_element_type=F32)
        prs, dens, lses = [], [], []
        for hd in range(n_heads):
            slope = slopes_ref[head0 + hd]
            s = s_all[hd * tq_sub:(hd + 1) * tq_sub]
            s = jnp.where(valid, s - slope * distf, NEG)
            m = jnp.max(s, axis=-1, keepdims=True)
            if use_sink:
                sink = sink_ref[head0 + hd]
                m = jnp.maximum(m, sink)
            pr = jnp.exp2(s - m)
            den = jnp.sum(pr, axis=-1, keepdims=True)
            if use_sink:
                den = den + jnp.exp2(sink - m)
            prs.append(pr.astype(BF16))
            dens.append(den)
            lses.append(m + jnp.log2(den))
        acc_all = jnp.dot(jnp.concatenate(prs, axis=0), vw, preferred_element_type=F32)
        for u in range(n_u):
            r0 = acc_all[(2 * u) * tq_sub:(2 * u + 1) * tq_sub] / dens[2 * u]
            r1 = acc_all[(2 * u + 1) * tq_sub:(2 * u + 2) * tq_sub] / dens[2 * u + 1]
            cols = slice(u * LANES, (u + 1) * LANES)
            o_ref[0, rows, cols] = jnp.where(lane < HEAD_DIM, r0, r1).astype(o_ref.dtype)
            if emit_lse:
                out_refs[1][0, rows, cols] = jnp.where(lane < HEAD_DIM, lses[2 * u], lses[2 * u + 1])


def _banded_call(qkv, q_blk, k_blk, v_blk, half, dil, n_groups, n_u, slopes, sinks, use_sink, emit_lse):
    b, _, sd, _ = qkv.shape
    tq_sub = 128
    n_sub = min(4, sd // tq_sub) if dil == 1 else dil
    tq_stream = tq_sub * n_sub // dil
    wq = n_u * LANES
    assert dil == 1 or (n_u == 1 and emit_lse)
    out_dtype = F32 if emit_lse else BF16
    out_spec = pl.BlockSpec((1, tq_sub * n_sub, wq), lambda bi, p, i: (bi, i, p))
    out_shape = [jax.ShapeDtypeStruct((b, sd * dil, n_groups * wq), out_dtype)]
    out_specs = [out_spec]
    if emit_lse:
        out_shape.append(out_shape[0])
        out_specs.append(out_spec)
    return pl.pallas_call(
        functools.partial(_banded_kernel, half=half, tq_sub=tq_sub, n_sub=n_sub, n_u=n_u, dil=dil,
                          seq=sd, use_sink=use_sink, emit_lse=emit_lse),
        grid=(b, n_groups, sd // tq_stream),
        in_specs=[
            _smem_spec(), _smem_spec(),
            pl.BlockSpec((1, dil, tq_stream, wq), lambda bi, p, i: (bi, 0, i, q_blk(p))),
            pl.BlockSpec((1, dil, sd, LANES), lambda bi, p, i: (bi, 0, 0, k_blk(p))),
            pl.BlockSpec((1, dil, sd, LANES), lambda bi, p, i: (bi, 0, 0, v_blk(p))),
        ],
        out_specs=out_specs,
        out_shape=out_shape,
        compiler_params=_cparams("parallel", "parallel", "arbitrary"),
        name="banded_attn",
    )(slopes, sinks, qkv, qkv, qkv)


def _nbr_kernel(q_ref, k_ref, v_ref, t_ref, o_ref, *, rows_per_tile, n_rows):
    i = pl.program_id(2)
    nkeys = WIN_R * GRID_W
    lane = lax.broadcasted_iota(jnp.int32, (GRID_W, LANES), 1)
    for rr in range(rows_per_tile):
        r = i * rows_per_tile + rr
        rs = jnp.clip(r - WIN_R // 2, 0, n_rows - WIN_R)
        off = rs - r + (WIN_R - 1)
        start = pl.multiple_of(rs * GRID_W, GRID_W)
        kw = k_ref[0, pl.ds(start, nkeys), :]
        vw = v_ref[0, pl.ds(start, nkeys), :]
        q = q_ref[0, rr * GRID_W:(rr + 1) * GRID_W, :]
        q2 = jnp.concatenate([_head_select(q, lane, 0), _head_select(q, lane, 1)], axis=0)
        s_all = lax.dot_general(q2, kw, _NT, preferred_element_type=F32)
        prs, dens = [], []
        for hh in range(2):
            s = s_all[hh * GRID_W:(hh + 1) * GRID_W] + t_ref[0, hh, off]
            m = jnp.max(s, axis=-1, keepdims=True)
            pr = jnp.exp2(s - m)
            dens.append(jnp.sum(pr, axis=-1, keepdims=True))
            prs.append(pr.astype(BF16))
        acc = jnp.dot(jnp.concatenate(prs, axis=0), vw, preferred_element_type=F32)
        o_ref[0, rr * GRID_W:(rr + 1) * GRID_W, :] = jnp.where(
            lane < HEAD_DIM, acc[:GRID_W] / dens[0], acc[GRID_W:] / dens[1]).astype(o_ref.dtype)


def _nbr_bias_table(rel_bias):
    col = jnp.arange(GRID_W)
    cs = jnp.clip(col - WIN_C // 2, 0, GRID_W - WIN_C)
    col_valid = (col[None, :] >= cs[:, None]) & (col[None, :] < cs[:, None] + WIN_C)
    dc_idx = jnp.clip(col[None, :] - col[:, None] + WIN_C - 1, 0, 2 * WIN_C - 2)
    dr = jnp.arange(WIN_R)[:, None] + jnp.arange(WIN_R)[None, :]
    tab = rel_bias.astype(F32)[:, dr] * LOG2E
    tab = tab[..., dc_idx]
    tab = jnp.where(col_valid, tab, NEG)
    tab = jnp.transpose(tab, (0, 1, 3, 2, 4))
    h = tab.shape[0]
    return tab.reshape(h // 2, 2, WIN_R, GRID_W, WIN_R * GRID_W)


def _nbr_call(qkv, table):
    b, s, _ = qkv.shape
    n_tiles = B_HEADS // 2
    n_rows = s // GRID_W
    rows_per_tile = min(8, n_rows)
    tq = rows_per_tile * GRID_W
    kv = lambda off: pl.BlockSpec((1, s, LANES), lambda bi, p, i: (bi, 0, off + p))
    return pl.pallas_call(
        functools.partial(_nbr_kernel, rows_per_tile=rows_per_tile, n_rows=n_rows),
        grid=(b, n_tiles, s // tq),
        in_specs=[
            pl.BlockSpec((1, tq, LANES), lambda bi, p, i: (bi, i, p)),
            kv(n_tiles), kv(2 * n_tiles),
            pl.BlockSpec((1, 2, WIN_R, GRID_W, WIN_R * GRID_W), lambda bi, p, i: (p, 0, 0, 0, 0)),
        ],
        out_specs=pl.BlockSpec((1, tq, LANES), lambda bi, p, i: (bi, i, p)),
        out_shape=jax.ShapeDtypeStruct((b, s, n_tiles * LANES), BF16),
        compiler_params=_cparams("parallel", "parallel", "arbitrary"),
        name="nbr_attn",
    )(qkv, qkv, qkv, table)


_POS_SPLIT = 32


def _diff_kernel(slopes_ref, lam_ref, q_ref, k_ref, vt_ref, kp_ref, g_ref, o_ref,
                 qa_s, m_s, acc_s, s_s, p_s, al_s, rb_s, mx_s, *, t, n_tiles, post_scale):
    h = pl.program_id(1)
    i = pl.program_id(2)
    slope = slopes_ref[h]
    ones_rows = jnp.ones((BF16_ROWS, t), BF16)

    q = q_ref[0]
    lane = lax.broadcasted_iota(jnp.int32, (t, LANES), 1)
    c = jnp.full((t, LANES), -slope, F32)
    c_hi = c.astype(BF16).astype(F32)
    aug = jnp.where(lane < 2, c_hi, jnp.where(lane < 4, c - c_hi, 0.0)).astype(BF16)
    for mp in range(2):
        qa_s[mp, :, :LANES] = _head_select(q, lane, mp)
        qa_s[mp, :, LANES:] = aug

    m_s[...] = jnp.full(m_s.shape, NEG, F32)
    acc_s[...] = jnp.zeros(acc_s.shape, F32)

    def tile_of(pos):
        if pos == 0:
            return i, None
        after = (pos - 1) >= i
        return pos - 1 + jnp.where(after, 1, 0), after

    def stage_scores(pos, slot):
        j, after = tile_of(pos)
        kt = k_ref[0, pl.ds(pl.multiple_of(j * t, t), t), :]
        if pos == 0:
            kpos = lax.broadcasted_iota(jnp.int32, (t, t), 0)
            qpos = lax.broadcasted_iota(jnp.int32, (t, t), 1)
            bias = slope * jnp.abs(kpos - qpos).astype(F32)
            rb_s[slot] = jnp.zeros((1, t), F32)
        else:
            ka = jnp.concatenate([kt, kp_ref[jnp.where(after, 0, 1)]], axis=1)
            sgn = jnp.where(after, 1.0, -1.0).astype(F32)
            qrow = lax.broadcasted_iota(jnp.int32, (1, t), 1).astype(F32)
            blk = (jnp.abs(j - i) * t).astype(F32)
            rb_s[slot] = slope * (sgn * qrow - blk)
        for mp in range(2):
            if pos == 0:
                s = lax.dot_general(kt, qa_s[mp, :, :LANES], _NT, preferred_element_type=F32) - bias
            else:
                s = lax.dot_general(ka, qa_s[mp], _NT, preferred_element_type=F32)
            s_s[slot, mp] = s
            mx_s[slot, mp] = jnp.max(s, axis=0, keepdims=True)

    def stage_softmax(slot):
        rb = rb_s[slot]
        for mp in range(2):
            m_old = m_s[mp]
            m_new = jnp.maximum(m_old, mx_s[slot, mp] + rb)
            al_s[slot, mp] = jnp.exp2(m_old - m_new)
            p_s[slot, mp] = jnp.exp2(s_s[slot, mp] - (m_new - rb)).astype(BF16)
            m_s[mp] = m_new

    def stage_values(pos, slot):
        j, _ = tile_of(pos)
        va = jnp.concatenate([vt_ref[0, 0, j], ones_rows], axis=0)
        for mp in range(2):
            acc_s[mp] = al_s[slot, mp] * acc_s[mp] + jnp.dot(va, p_s[slot, mp], preferred_element_type=F32)

    for n in range(n_tiles + 2):
        if 2 <= n:
            stage_values(n - 2, n % 2)
        if 1 <= n <= n_tiles:
            stage_softmax(1 - n % 2)
        if n < n_tiles:
            stage_scores(n, n % 2)

    o0 = acc_s[0, :LANES] / acc_s[0, LANES:LANES + 1]
    o1 = acc_s[1, :LANES] / acc_s[1, LANES:LANES + 1]
    o = o0 - lam_ref[0] * o1
    ms = jnp.mean(o * o, axis=0, keepdims=True)
    y = (o * lax.rsqrt(ms + EPS) * g_ref[...]) * post_scale
    o_ref[0] = y.T.astype(o_ref.dtype)


def _key_offset_tiles(t):
    r = jnp.arange(t)
    hi = (r // _POS_SPLIT * _POS_SPLIT).astype(F32)
    lo = (r % _POS_SPLIT).astype(F32)
    tile = jnp.zeros((t, LANES), F32)
    tile = tile.at[:, 0].set(hi).at[:, 1].set(lo).at[:, 2].set(hi).at[:, 3].set(lo)
    return jnp.stack([tile, -tile]).astype(BF16)


def _diff_tile(s):
    return min(512, s)


def _diff_call(qk, vt, slopes, lam, subln, post_scale):
    b, s, _ = qk.shape
    t = _diff_tile(s)
    n_tiles = s // t
    assert t <= _POS_SPLIT * BF16_ROWS
    return pl.pallas_call(
        functools.partial(_diff_kernel, t=t, n_tiles=n_tiles, post_scale=post_scale),
        grid=(b, C_HEADS, n_tiles),
        in_specs=[
            _smem_spec(), _smem_spec(),
            pl.BlockSpec((1, t, LANES), lambda bi, h, i: (bi, i, h)),
            pl.BlockSpec((1, s, LANES), lambda bi, h, i: (bi, 0, C_HEADS + h)),
            pl.BlockSpec((1, 1, n_tiles, LANES, t), lambda bi, h, i: (bi, h, 0, 0, 0)),
            pl.BlockSpec((2, t, LANES), lambda bi, h, i: (0, 0, 0)),
            pl.BlockSpec((LANES, 1), lambda bi, h, i: (0, 0)),
        ],
        out_specs=pl.BlockSpec((1, t, LANES), lambda bi, h, i: (bi, i, h)),
        out_shape=jax.ShapeDtypeStruct((b, s, C_HEADS * LANES), BF16),
        scratch_shapes=[
            pltpu.VMEM((2, t, 2 * LANES), BF16),
            pltpu.VMEM((2, 1, t), F32),
            pltpu.VMEM((2, LANES + BF16_ROWS, t), F32),
            pltpu.VMEM((2, 2, t, t), F32),
            pltpu.VMEM((2, 2, t, t), BF16),
            pltpu.VMEM((2, 2, 1, t), F32),
            pltpu.VMEM((2, 1, t), F32),
            pltpu.VMEM((2, 2, 1, t), F32),
        ],
        compiler_params=_cparams("parallel", "parallel", "parallel"),
        name="diff_attn",
    )(slopes, lam, qk, qk, vt, _key_offset_tiles(t), subln.astype(F32).reshape(LANES, 1))


def _mixer_a(x, g, sc, sh, wqkv, sink):
    d = x.shape[-1]
    nq = A_HEADS * HEAD_DIM
    nkv = A_KV_HEADS * HEAD_DIM

    def dup(wpart):
        wpart = wpart.reshape(d, A_KV_HEADS, 1, HEAD_DIM)
        return jnp.broadcast_to(wpart, (d, A_KV_HEADS, 2, HEAD_DIM)).reshape(d, 2 * nkv).astype(BF16)

    w = jnp.concatenate([_scaled_q_weights(wqkv[:, :nq]), dup(wqkv[:, nq:nq + nkv]),
                         dup(wqkv[:, nq + nkv:])], axis=1)
    qkv = _qkv_call(x, g, sc, sh, w)
    q_tiles = nq // LANES
    kv_tiles = 2 * nkv // LANES
    pairs_per_kv = A_HEADS // A_KV_HEADS // 2
    return _banded_call(
        qkv, q_blk=lambda kv: kv, k_blk=lambda kv: q_tiles + kv, v_blk=lambda kv: q_tiles + kv_tiles + kv,
        half=A_WINDOW, dil=1, n_groups=A_KV_HEADS, n_u=pairs_per_kv, slopes=_alibi_slopes(A_HEADS) * LOG2E,
        sinks=sink.astype(F32) * LOG2E, use_sink=True, emit_lse=False), []


def _mixer_b(x, g, sc, sh, wqkv, rel_bias):
    nq = B_HEADS * HEAD_DIM
    w = jnp.concatenate([_scaled_q_weights(wqkv[:, :nq]), wqkv[:, nq:].astype(BF16)], axis=1)
    qkv = _qkv_call(x, g, sc, sh, w)[:, 0]
    return [_nbr_call(qkv, _nbr_bias_table(rel_bias))], []


def _mixer_c(x, g, sc, sh, wqkv, lam, subln, lam_init):
    nq = 2 * C_HEADS * HEAD_DIM
    w = jnp.concatenate([_scaled_q_weights(wqkv[:, :nq]), wqkv[:, nq:2 * nq].astype(BF16)], axis=1)
    qk, vt = _qkv_call(x, g, sc, sh, w, wqkv[:, 2 * nq:].T.astype(BF16), tile=_diff_tile(x.shape[1]))
    qk = qk[:, 0]
    lf = lam.astype(F32)
    lam_full = jnp.exp(jnp.sum(lf[0] * lf[1])) - jnp.exp(jnp.sum(lf[2] * lf[3])) + lam_init
    return [_diff_call(qk, vt, _alibi_slopes(C_HEADS) * LOG2E, lam_full.reshape(1), subln,
                       1.0 - lam_init)], []


def _mixer_d(x, g, sc, sh, wqkv):
    d = x.shape[-1]
    nh = D_HEADS * HEAD_DIM
    n_tiles = D_HEADS // 2
    outs, lses = [], []
    for gi, (win, dil) in enumerate(D_GROUPS):
        wg = wqkv[:, gi * 3 * nh:(gi + 1) * 3 * nh]
        w = jnp.concatenate([_scaled_q_weights(wg[:, :nh]), wg[:, nh:].astype(BF16)], axis=1)
        qkv = _qkv_call(x, g, sc, sh, w, dil=dil, tile=256 if dil == 1 else 512)
        o, lse = _banded_call(
            qkv, q_blk=lambda p: p, k_blk=lambda p: n_tiles + p, v_blk=lambda p: 2 * n_tiles + p,
            half=win // (2 * dil), dil=dil, n_groups=n_tiles, n_u=1,
            slopes=_alibi_slopes(D_HEADS) * (LOG2E * dil), sinks=jnp.zeros((D_HEADS,), F32),
            use_sink=False, emit_lse=True)
        outs.append(o)
        lses.append(lse)
    return outs, lses


def _trunk(x, mod, g_mix_pre, g_mix_post, g_mlp_pre, g_mlp_post, mlp_w1, mlp_w2,
           a_wqkv, a_wo, a_sink, b_wqkv, b_wo, b_rel_bias, c_wqkv, c_wo, c_lambda, c_subln, d_wqkv, d_wo):
    depth = mod.shape[0]
    d = x.shape[-1]
    for i in range(depth):
        sh1, sc1, gt1, sh2, sc2, gt2 = [mod[i, :, None, k * d:(k + 1) * d] for k in range(6)]
        kind, occ = i % N_MIXERS, i // N_MIXERS
        if kind == 0:
            (o, lse), wo = _mixer_a(x, g_mix_pre[i], sc1, sh1, a_wqkv[occ], a_sink[occ]), a_wo[occ]
        elif kind == 1:
            (o, lse), wo = _mixer_b(x, g_mix_pre[i], sc1, sh1, b_wqkv[occ], b_rel_bias[occ]), b_wo[occ]
        elif kind == 2:
            (o, lse), wo = _mixer_c(x, g_mix_pre[i], sc1, sh1, c_wqkv[occ], c_lambda[occ], c_subln[occ],
                                    0.8 - 0.6 * math.exp(-0.3 * i)), c_wo[occ]
        else:
            (o, lse), wo = _mixer_d(x, g_mix_pre[i], sc1, sh1, d_wqkv[occ]), d_wo[occ]
        x = _out_call(o, lse, wo.astype(BF16), x, gt1, g_mix_post[i])
        x = _mlp_call(x, g_mlp_pre[i], sc2, sh2, mlp_w1[i].astype(BF16), mlp_w2[i].astype(BF16),
                      gt2, g_mlp_post[i])
    return x


def kernel(x_prompt, x_sample, c_prompt, c_sample, ada_w, ada_b, g_mix_pre, g_mix_post, g_mlp_pre, g_mlp_post, mlp_w1, mlp_w2, a_wqkv, a_wo, a_sink, b_wqkv, b_wo, b_rel_bias, c_wqkv, c_wo, c_lambda, c_subln, d_wqkv, d_wo):
    bp, bs = c_prompt.shape[0], c_sample.shape[0]
    c_all = jnp.concatenate([c_prompt, c_sample], axis=0)
    pad = (-c_all.shape[0]) % 8
    c_all = jnp.pad(c_all, ((0, pad), (0, 0)))
    mod = _adaln_call(c_all, ada_w, ada_b)
    params = (g_mix_pre, g_mix_post, g_mlp_pre, g_mlp_post, mlp_w1, mlp_w2, a_wqkv, a_wo, a_sink,
              b_wqkv, b_wo, b_rel_bias, c_wqkv, c_wo, c_lambda, c_subln, d_wqkv, d_wo)
    y_prompt = _trunk(x_prompt, mod[:, :bp], *params)
    y_sample = _trunk(x_sample, mod[:, bp:bp + bs], *params)
    return (y_prompt, y_sample)
```

```python
import functools
import math

import jax
import jax.numpy as jnp
from jax import lax
from jax.experimental import pallas as pl
from jax.experimental.pallas import tpu as pltpu

F32 = jnp.float32
BF16 = jnp.bfloat16

D_MODEL = 1024
HEAD_DIM = 64
LANES = 128
BF16_ROWS = 16
D_FF = 4 * D_MODEL
EPS = 1e-6
NEG = -1e30
LOG2E = math.log2(math.e)
Q_SCALE = HEAD_DIM ** -0.5 * LOG2E
N_MIXERS = 4

A_HEADS, A_KV_HEADS, A_WINDOW = 16, 4, 128
B_HEADS, GRID_W, WIN_R, WIN_C = 16, 64, 8, 16
C_HEADS = 8
D_GROUPS = ((128, 1), (512, 4), (2048, 16))
D_HEADS = 8

VMEM_LIMIT_BYTES = 56 * 1024 * 1024

_NT = (((1,), (1,)), ((), ()))


def _cparams(*sem):
    return pltpu.CompilerParams(dimension_semantics=sem, vmem_limit_bytes=VMEM_LIMIT_BYTES)


def _smem_spec():
    return pl.BlockSpec(memory_space=pltpu.SMEM)


def _alibi_slopes(n):
    return 2.0 ** (-8.0 * jnp.arange(1, n + 1, dtype=F32) / n)


def _modulated_norm(x, g, sc, sh):
    ms = jnp.mean(x * x, axis=-1, keepdims=True)
    return (x * lax.rsqrt(ms + EPS) * g) * (1.0 + sc) + sh


def _gated_norm_residual(x, h, g, gt):
    ms = jnp.mean(h * h, axis=-1, keepdims=True)
    return x + gt * (h * lax.rsqrt(ms + EPS) * g)


def _head_select(q, lane, hh):
    keep = (lane >= HEAD_DIM) if hh else (lane < HEAD_DIM)
    return jnp.where(keep, q, jnp.zeros_like(q))


def _scaled_q_weights(wq):
    return (wq * Q_SCALE).astype(BF16)


def _adaln_kernel(c_ref, w_ref, b_ref, o_ref):
    c = c_ref[...]
    ca = (c * jax.nn.sigmoid(c)).astype(BF16)
    o_ref[0] = jnp.dot(ca, w_ref[0].astype(BF16), preferred_element_type=F32) + b_ref[0]


def _adaln_call(c_pad, ada_w, ada_b):
    depth, d, n = ada_w.shape
    bp = c_pad.shape[0]
    tn = 1536
    return pl.pallas_call(
        _adaln_kernel,
        grid=(depth, n // tn),
        in_specs=[
            pl.BlockSpec((bp, d), lambda l, j: (0, 0)),
            pl.BlockSpec((1, d, tn), lambda l, j: (l, 0, j)),
            pl.BlockSpec((1, 1, tn), lambda l, j: (l, 0, j)),
        ],
        out_specs=pl.BlockSpec((1, bp, tn), lambda l, j: (l, 0, j)),
        out_shape=jax.ShapeDtypeStruct((depth, bp, n), F32),
        compiler_params=_cparams("parallel", "parallel"),
        name="adaln",
    )(c_pad, ada_w, ada_b.reshape(depth, 1, n))


def _qkv_kernel(*refs, n_chunk, has_t, dil):
    if has_t:
        x_ref, g_ref, sc_ref, sh_ref, w_ref, wt_ref, o_ref, ot_ref = refs
    elif dil > 1:
        x_ref, g_ref, sc_ref, sh_ref, w_ref, o_ref, slab_s, h_s = refs
    else:
        x_ref, g_ref, sc_ref, sh_ref, w_ref, o_ref = refs
    tm = o_ref.shape[2]
    hn = _modulated_norm(x_ref[0], g_ref[...], sc_ref[0], sh_ref[0])
    if dil == 1:
        h = hn.astype(BF16)
    else:
        for c in range(slab_s.shape[0]):
            slab_s[c] = hn[:, c * LANES:(c + 1) * LANES]
        for rho in range(dil):
            for c in range(slab_s.shape[0]):
                h_s[rho * tm:(rho + 1) * tm, c * LANES:(c + 1) * LANES] = (
                    slab_s[c, pl.ds(rho, tm, stride=dil), :].astype(BF16))
        h = h_s[...]
    n = w_ref.shape[1]
    for c0 in range(0, n, n_chunk):
        oc = jnp.dot(h, w_ref[:, c0:c0 + n_chunk], preferred_element_type=F32).astype(o_ref.dtype)
        o_ref[0, :, :, c0:c0 + n_chunk] = oc.reshape(dil, tm, n_chunk)
    if has_t:
        ht = lax.dot_general(wt_ref[...], h, _NT, preferred_element_type=F32).astype(ot_ref.dtype)
        ot_ref[0, :, 0] = ht.reshape(ot_ref.shape[1], LANES, ht.shape[1])


def _qkv_call(x, g, sc, sh, w, wt=None, dil=1, tile=512):
    b, s, d = x.shape
    n = w.shape[1]
    sd = s // dil
    tile = min(tile, s)
    tm = tile // dil
    assert tm % BF16_ROWS == 0
    in_specs = [
        pl.BlockSpec((1, tile, d), lambda bi, i: (bi, i, 0)),
        pl.BlockSpec((1, d), lambda bi, i: (0, 0)),
        pl.BlockSpec((1, 1, d), lambda bi, i: (bi, 0, 0)),
        pl.BlockSpec((1, 1, d), lambda bi, i: (bi, 0, 0)),
        pl.BlockSpec((d, n), lambda bi, i: (0, 0)),
    ]
    out_shape = [jax.ShapeDtypeStruct((b, dil, sd, n), BF16)]
    out_specs = [pl.BlockSpec((1, dil, tm, n), lambda bi, i: (bi, 0, i, 0))]
    args = [x, g.reshape(1, d), sc, sh, w]
    scratch = []
    if wt is not None:
        assert dil == 1
        nt = wt.shape[0]
        in_specs.append(pl.BlockSpec((nt, d), lambda bi, i: (0, 0)))
        out_shape.append(jax.ShapeDtypeStruct((b, nt // LANES, s // tile, LANES, tile), BF16))
        out_specs.append(pl.BlockSpec((1, nt // LANES, 1, LANES, tile), lambda bi, i: (bi, 0, i, 0, 0)))
        args.append(wt)
    if dil > 1:
        scratch += [pltpu.VMEM((d // LANES, tile, LANES), F32), pltpu.VMEM((tile, d), BF16)]
    outs = pl.pallas_call(
        functools.partial(_qkv_kernel, n_chunk=512, has_t=wt is not None, dil=dil),
        grid=(b, s // tile),
        in_specs=in_specs,
        out_specs=out_specs,
        out_shape=out_shape,
        scratch_shapes=scratch,
        compiler_params=_cparams("parallel", "parallel"),
        name="qkv_proj",
    )(*args)
    return (outs[0], outs[1]) if wt is not None else outs[0]


def _merge_groups(o_refs, lse_refs):
    m = lse_refs[0][0]
    for r in lse_refs[1:]:
        m = jnp.maximum(m, r[0])
    num = jnp.zeros(m.shape, F32)
    den = jnp.zeros(m.shape, F32)
    for o_r, l_r in zip(o_refs, lse_refs):
        e = jnp.exp2(l_r[0] - m)
        den = den + e
        num = num + e * o_r[0]
    return num / den


def _tail_kernel(*refs, n_mix, f_chunk):
    n_o = max(n_mix, 1)
    o_refs, lse_refs = refs[:n_o], refs[n_o:n_o + n_mix]
    (wo_ref, x_ref, gt1_ref, g1_ref, g2_ref, sc_ref, sh_ref, w1_ref, w2_ref, gt2_ref, g3_ref,
     y_ref) = refs[n_o + n_mix:]
    o = _merge_groups(o_refs, lse_refs).astype(BF16) if n_mix else o_refs[0][0]
    x1 = _gated_norm_residual(x_ref[0], jnp.dot(o, wo_ref[...], preferred_element_type=F32),
                              g1_ref[...], gt1_ref[0])
    h = _modulated_norm(x1, g2_ref[...], sc_ref[0], sh_ref[0]).astype(BF16)
    acc = jnp.zeros(x1.shape, F32)
    for c0 in range(0, w1_ref.shape[1], f_chunk):
        a = jnp.dot(h, w1_ref[:, c0:c0 + f_chunk], preferred_element_type=F32)
        a = jnp.maximum(a, 0.0)
        a = (a * a).astype(BF16)
        acc = acc + jnp.dot(a, w2_ref[c0:c0 + f_chunk, :], preferred_element_type=F32)
    y_ref[0] = _gated_norm_residual(x1, acc, g3_ref[...], gt2_ref[0])


def _tail_call(o_list, lse_list, wo, x, gt1, g1, g2, sc2, sh2, w1, w2, gt2, g3):
    b, s, d = x.shape
    kd = wo.shape[0]
    f = w1.shape[1]
    n_mix = len(lse_list)
    tm = min(256 if n_mix else 512, s)
    tile = pl.BlockSpec((1, tm, kd), lambda bi, i: (bi, i, 0))
    row = pl.BlockSpec((1, tm, d), lambda bi, i: (bi, i, 0))
    vec = pl.BlockSpec((1, 1, d), lambda bi, i: (bi, 0, 0))
    gain = pl.BlockSpec((1, d), lambda bi, i: (0, 0))

    def resident(shape):
        return pl.BlockSpec(shape, lambda bi, i: (0, 0), pipeline_mode=pl.Buffered(1))

    return pl.pallas_call(
        functools.partial(_tail_kernel, n_mix=n_mix, f_chunk=1024),
        grid=(b, s // tm),
        in_specs=[tile] * (len(o_list) + n_mix) + [
            resident((kd, d)), row, vec, gain, gain, vec, vec, resident((d, f)), resident((f, d)), vec, gain],
        out_specs=row,
        out_shape=jax.ShapeDtypeStruct((b, s, d), F32),
        compiler_params=_cparams("parallel", "parallel"),
        name="layer_tail",
    )(*o_list, *lse_list, wo, x, gt1, g1.reshape(1, d), g2.reshape(1, d), sc2, sh2, w1, w2, gt2,
      g3.reshape(1, d))


def _banded_kernel(slopes_ref, sink_ref, q_ref, k_ref, v_ref, *out_refs,
                   half, tq_sub, n_sub, n_u, dil, seq, use_sink, emit_lse):
    o_ref = out_refs[0]
    n_heads = 2 * n_u
    head0 = pl.program_id(1) * n_heads
    span = tq_sub + 2 * half
    lane = lax.broadcasted_iota(jnp.int32, (tq_sub, LANES), 1)
    base = (lax.broadcasted_iota(jnp.int32, (tq_sub, span), 1)
            - lax.broadcasted_iota(jnp.int32, (tq_sub, span), 0) - half)
    def block(sb):
        if dil == 1:
            r, q0 = 0, (pl.program_id(2) * n_sub + sb) * tq_sub
            qrows = slice(sb * tq_sub, (sb + 1) * tq_sub)
            rows = qrows
        else:
            r, q0 = sb, pl.program_id(2) * tq_sub
            qrows = slice(0, tq_sub)
            rows = pl.ds(r, tq_sub, stride=dil)
        start = pl.multiple_of(jnp.clip(q0 - half, 0, seq - span), HEAD_DIM)
        return r, qrows, rows, q0, start

    def scores(sb):
        r, qrows, _, _, start = block(sb)
        qs = []
        for u in range(n_u):
            qu = q_ref[0, r, qrows, u * LANES:(u + 1) * LANES]
            qs += [_head_select(qu, lane, 0), _head_select(qu, lane, 1)]
        return lax.dot_general(jnp.concatenate(qs, axis=0), k_ref[0, r, pl.ds(start, span), :], _NT,
                               preferred_element_type=F32)

    def finish(sb, s_all):
        r, _, rows, q0, start = block(sb)
        vw = v_ref[0, r, pl.ds(start, span), :]
        dist = jnp.abs(base + (start - q0 + half))
        valid = dist <= half
        distf = dist.astype(F32)
        prs, dens, lses = [], [], []
        for hd in range(n_heads):
            slope = slopes_ref[head0 + hd]
            s = s_all[hd * tq_sub:(hd + 1) * tq_sub]
            s = jnp.where(valid, s - slope * distf, NEG)
            m = jnp.max(s, axis=-1, keepdims=True)
            if use_sink:
                sink = sink_ref[head0 + hd]
                m = jnp.maximum(m, sink)
            pr = jnp.exp2(s - m)
            den = jnp.sum(pr, axis=-1, keepdims=True)
            if use_sink:
                den = den + jnp.exp2(sink - m)
            prs.append(pr.astype(BF16))
            dens.append(den)
            lses.append(m + jnp.log2(den))
        acc_all = jnp.dot(jnp.concatenate(prs, axis=0), vw, preferred_element_type=F32)
        for u in range(n_u):
            r0 = acc_all[(2 * u) * tq_sub:(2 * u + 1) * tq_sub] / dens[2 * u]
            r1 = acc_all[(2 * u + 1) * tq_sub:(2 * u + 2) * tq_sub] / dens[2 * u + 1]
            cols = slice(u * LANES, (u + 1) * LANES)
            o_ref[0, rows, cols] = jnp.where(lane < HEAD_DIM, r0, r1).astype(o_ref.dtype)
            if emit_lse:
                out_refs[1][0, rows, cols] = jnp.where(lane < HEAD_DIM, lses[2 * u], lses[2 * u + 1])

    ahead = min(2, n_sub)
    pending = [scores(sb) for sb in range(ahead)]
    for sb in range(n_sub):
        if sb + ahead < n_sub:
            pending.append(scores(sb + ahead))
        finish(sb, pending.pop(0))


def _banded_call(qkv, q_blk, k_blk, v_blk, half, dil, n_groups, n_u, slopes, sinks, use_sink, emit_lse):
    b, _, sd, _ = qkv.shape
    tq_sub = 128
    n_sub = min(4, sd // tq_sub) if dil == 1 else dil
    tq_stream = tq_sub * n_sub // dil
    wq = n_u * LANES
    assert dil == 1 or (n_u == 1 and emit_lse)
    out_dtype = F32 if emit_lse else BF16
    out_spec = pl.BlockSpec((1, tq_sub * n_sub, wq), lambda bi, p, i: (bi, i, p))
    out_shape = [jax.ShapeDtypeStruct((b, sd * dil, n_groups * wq), out_dtype)]
    out_specs = [out_spec]
    if emit_lse:
        out_shape.append(out_shape[0])
        out_specs.append(out_spec)
    return pl.pallas_call(
        functools.partial(_banded_kernel, half=half, tq_sub=tq_sub, n_sub=n_sub, n_u=n_u, dil=dil,
                          seq=sd, use_sink=use_sink, emit_lse=emit_lse),
        grid=(b, n_groups, sd // tq_stream),
        in_specs=[
            _smem_spec(), _smem_spec(),
            pl.BlockSpec((1, dil, tq_stream, wq), lambda bi, p, i: (bi, 0, i, q_blk(p))),
            pl.BlockSpec((1, dil, sd, LANES), lambda bi, p, i: (bi, 0, 0, k_blk(p))),
            pl.BlockSpec((1, dil, sd, LANES), lambda bi, p, i: (bi, 0, 0, v_blk(p))),
        ],
        out_specs=out_specs,
        out_shape=out_shape,
        compiler_params=_cparams("parallel", "parallel", "arbitrary"),
        name="banded_attn",
    )(slopes, sinks, qkv, qkv, qkv)


def _nbr_kernel(q_ref, k_ref, v_ref, t_ref, o_ref, *, rows_per_tile, n_rows):
    i = pl.program_id(2)
    nkeys = WIN_R * GRID_W
    lane = lax.broadcasted_iota(jnp.int32, (GRID_W, LANES), 1)
    def band(rr):
        r = i * rows_per_tile + rr
        rs = jnp.clip(r - WIN_R // 2, 0, n_rows - WIN_R)
        off = rs - r + (WIN_R - 1)
        return pl.multiple_of(rs * GRID_W, GRID_W), off

    def scores(rr):
        start, _ = band(rr)
        q = q_ref[0, rr * GRID_W:(rr + 1) * GRID_W, :]
        q2 = jnp.concatenate([_head_select(q, lane, 0), _head_select(q, lane, 1)], axis=0)
        return lax.dot_general(q2, k_ref[0, pl.ds(start, nkeys), :], _NT, preferred_element_type=F32)

    def finish(rr, s_all):
        start, off = band(rr)
        prs, dens = [], []
        for hh in range(2):
            s = s_all[hh * GRID_W:(hh + 1) * GRID_W] + t_ref[0, hh, off]
            m = jnp.max(s, axis=-1, keepdims=True)
            pr = jnp.exp2(s - m)
            dens.append(jnp.sum(pr, axis=-1, keepdims=True))
            prs.append(pr.astype(BF16))
        acc = jnp.dot(jnp.concatenate(prs, axis=0), v_ref[0, pl.ds(start, nkeys), :],
                      preferred_element_type=F32)
        o_ref[0, rr * GRID_W:(rr + 1) * GRID_W, :] = jnp.where(
            lane < HEAD_DIM, acc[:GRID_W] / dens[0], acc[GRID_W:] / dens[1]).astype(o_ref.dtype)

    ahead = min(2, rows_per_tile)
    pending = [scores(rr) for rr in range(ahead)]
    for rr in range(rows_per_tile):
        if rr + ahead < rows_per_tile:
            pending.append(scores(rr + ahead))
        finish(rr, pending.pop(0))


def _nbr_bias_table(rel_bias):
    col = jnp.arange(GRID_W)
    cs = jnp.clip(col - WIN_C // 2, 0, GRID_W - WIN_C)
    col_valid = (col[None, :] >= cs[:, None]) & (col[None, :] < cs[:, None] + WIN_C)
    dc_idx = jnp.clip(col[None, :] - col[:, None] + WIN_C - 1, 0, 2 * WIN_C - 2)
    dr = jnp.arange(WIN_R)[:, None] + jnp.arange(WIN_R)[None, :]
    tab = rel_bias.astype(F32)[:, dr] * LOG2E
    tab = tab[..., dc_idx]
    tab = jnp.where(col_valid, tab, NEG)
    tab = jnp.transpose(tab, (0, 1, 3, 2, 4))
    h = tab.shape[0]
    return tab.reshape(h // 2, 2, WIN_R, GRID_W, WIN_R * GRID_W)


def _nbr_call(qkv, table):
    b, s, _ = qkv.shape
    n_tiles = B_HEADS // 2
    n_rows = s // GRID_W
    rows_per_tile = min(8, n_rows)
    tq = rows_per_tile * GRID_W
    kv = lambda off: pl.BlockSpec((1, s, LANES), lambda bi, p, i: (bi, 0, off + p))
    return pl.pallas_call(
        functools.partial(_nbr_kernel, rows_per_tile=rows_per_tile, n_rows=n_rows),
        grid=(b, n_tiles, s // tq),
        in_specs=[
            pl.BlockSpec((1, tq, LANES), lambda bi, p, i: (bi, i, p)),
            kv(n_tiles), kv(2 * n_tiles),
            pl.BlockSpec((1, 2, WIN_R, GRID_W, WIN_R * GRID_W), lambda bi, p, i: (p, 0, 0, 0, 0)),
        ],
        out_specs=pl.BlockSpec((1, tq, LANES), lambda bi, p, i: (bi, i, p)),
        out_shape=jax.ShapeDtypeStruct((b, s, n_tiles * LANES), BF16),
        compiler_params=_cparams("parallel", "parallel", "arbitrary"),
        name="nbr_attn",
    )(qkv, qkv, qkv, table)


_POS_SPLIT = 32


def _diff_kernel(slopes_ref, lam_ref, q_ref, k_ref, vt_ref, kp_ref, g_ref, o_ref,
                 qa_s, m_s, acc_s, s_s, p_s, al_s, rb_s, mx_s, *, t, n_tiles, post_scale):
    h = pl.program_id(1)
    i = pl.program_id(2)
    slope = slopes_ref[h]
    ones_rows = jnp.ones((BF16_ROWS, t), BF16)

    q = q_ref[0]
    lane = lax.broadcasted_iota(jnp.int32, (t, LANES), 1)
    c = jnp.full((t, LANES), -slope, F32)
    c_hi = c.astype(BF16).astype(F32)
    aug = jnp.where(lane < 2, c_hi, jnp.where(lane < 4, c - c_hi, 0.0)).astype(BF16)
    for mp in range(2):
        qa_s[mp, :, :LANES] = _head_select(q, lane, mp)
        qa_s[mp, :, LANES:] = aug

    def tile_of(pos):
        if pos == 0:
            return i, None
        after = (pos - 1) >= i
        return pos - 1 + jnp.where(after, 1, 0), after

    m_s[...] = jnp.full(m_s.shape, NEG, F32)
    acc_s[...] = jnp.zeros(acc_s.shape, F32)

    def stage_scores(pos, slot):
        j, after = tile_of(pos)
        kt = k_ref[0, pl.ds(pl.multiple_of(j * t, t), t), :]
        if pos == 0:
            kpos = lax.broadcasted_iota(jnp.int32, (t, t), 0)
            qpos = lax.broadcasted_iota(jnp.int32, (t, t), 1)
            bias = slope * jnp.abs(kpos - qpos).astype(F32)
            rb_s[slot] = jnp.zeros((1, t), F32)
        else:
            ka = jnp.concatenate([kt, kp_ref[jnp.where(after, 0, 1)]], axis=1)
            sgn = jnp.where(after, 1.0, -1.0).astype(F32)
            qrow = lax.broadcasted_iota(jnp.int32, (1, t), 1).astype(F32)
            blk = (jnp.abs(j - i) * t).astype(F32)
            rb_s[slot] = slope * (sgn * qrow - blk)
        for mp in range(2):
            if pos == 0:
                s = lax.dot_general(kt, qa_s[mp, :, :LANES], _NT, preferred_element_type=F32) - bias
            else:
                s = lax.dot_general(ka, qa_s[mp], _NT, preferred_element_type=F32)
            s_s[slot, mp] = s
            mx_s[slot, mp] = jnp.max(s, axis=0, keepdims=True)

    def stage_softmax(slot):
        rb = rb_s[slot]
        for mp in range(2):
            m_old = m_s[mp]
            m_new = jnp.maximum(m_old, mx_s[slot, mp] + rb)
            al_s[slot, mp] = jnp.exp2(m_old - m_new)
            p_s[slot, mp] = jnp.exp2(s_s[slot, mp] - (m_new - rb)).astype(BF16)
            m_s[mp] = m_new

    def stage_values(pos, slot):
        j, _ = tile_of(pos)
        va = jnp.concatenate([vt_ref[0, 0, j], ones_rows], axis=0)
        for mp in range(2):
            acc_s[mp] = al_s[slot, mp] * acc_s[mp] + jnp.dot(va, p_s[slot, mp], preferred_element_type=F32)

    for n in range(n_tiles + 2):
        if n < n_tiles:
            stage_scores(n, n % 2)
        if 2 <= n:
            stage_values(n - 2, n % 2)
        if 1 <= n <= n_tiles:
            stage_softmax(1 - n % 2)

    o0 = acc_s[0, :LANES] / acc_s[0, LANES:LANES + 1]
    o1 = acc_s[1, :LANES] / acc_s[1, LANES:LANES + 1]
    o = o0 - lam_ref[0] * o1
    ms = jnp.mean(o * o, axis=0, keepdims=True)
    y = (o * lax.rsqrt(ms + EPS) * g_ref[...]) * post_scale
    o_ref[0] = y.T.astype(o_ref.dtype)


def _key_offset_tiles(t):
    r = jnp.arange(t)
    hi = (r // _POS_SPLIT * _POS_SPLIT).astype(F32)
    lo = (r % _POS_SPLIT).astype(F32)
    tile = jnp.zeros((t, LANES), F32)
    tile = tile.at[:, 0].set(hi).at[:, 1].set(lo).at[:, 2].set(hi).at[:, 3].set(lo)
    return jnp.stack([tile, -tile]).astype(BF16)


def _diff_tile(s):
    return min(512, s)


def _diff_call(qk, vt, slopes, lam, subln, post_scale):
    b, s, _ = qk.shape
    t = _diff_tile(s)
    n_tiles = s // t
    assert t <= _POS_SPLIT * BF16_ROWS
    return pl.pallas_call(
        functools.partial(_diff_kernel, t=t, n_tiles=n_tiles, post_scale=post_scale),
        grid=(b, C_HEADS, n_tiles),
        in_specs=[
            _smem_spec(), _smem_spec(),
            pl.BlockSpec((1, t, LANES), lambda bi, h, i: (bi, i, h)),
            pl.BlockSpec((1, s, LANES), lambda bi, h, i: (bi, 0, C_HEADS + h)),
            pl.BlockSpec((1, 1, n_tiles, LANES, t), lambda bi, h, i: (bi, h, 0, 0, 0)),
            pl.BlockSpec((2, t, LANES), lambda bi, h, i: (0, 0, 0)),
            pl.BlockSpec((LANES, 1), lambda bi, h, i: (0, 0)),
        ],
        out_specs=pl.BlockSpec((1, t, LANES), lambda bi, h, i: (bi, i, h)),
        out_shape=jax.ShapeDtypeStruct((b, s, C_HEADS * LANES), BF16),
        scratch_shapes=[
            pltpu.VMEM((2, t, 2 * LANES), BF16),
            pltpu.VMEM((2, 1, t), F32),
            pltpu.VMEM((2, LANES + BF16_ROWS, t), F32),
            pltpu.VMEM((2, 2, t, t), F32),
            pltpu.VMEM((2, 2, t, t), BF16),
            pltpu.VMEM((2, 2, 1, t), F32),
            pltpu.VMEM((2, 1, t), F32),
            pltpu.VMEM((2, 2, 1, t), F32),
        ],
        compiler_params=_cparams("parallel", "parallel", "parallel"),
        name="diff_attn",
    )(slopes, lam, qk, qk, vt, _key_offset_tiles(t), subln.astype(F32).reshape(LANES, 1))


def _mixer_a(x, g, sc, sh, wqkv, sink):
    d = x.shape[-1]
    nq = A_HEADS * HEAD_DIM
    nkv = A_KV_HEADS * HEAD_DIM

    def dup(wpart):
        wpart = wpart.reshape(d, A_KV_HEADS, 1, HEAD_DIM)
        return jnp.broadcast_to(wpart, (d, A_KV_HEADS, 2, HEAD_DIM)).reshape(d, 2 * nkv).astype(BF16)

    w = jnp.concatenate([_scaled_q_weights(wqkv[:, :nq]), dup(wqkv[:, nq:nq + nkv]),
                         dup(wqkv[:, nq + nkv:])], axis=1)
    qkv = _qkv_call(x, g, sc, sh, w)
    q_tiles = nq // LANES
    kv_tiles = 2 * nkv // LANES
    pairs_per_kv = A_HEADS // A_KV_HEADS // 2
    return _banded_call(
        qkv, q_blk=lambda kv: kv, k_blk=lambda kv: q_tiles + kv, v_blk=lambda kv: q_tiles + kv_tiles + kv,
        half=A_WINDOW, dil=1, n_groups=A_KV_HEADS, n_u=pairs_per_kv, slopes=_alibi_slopes(A_HEADS) * LOG2E,
        sinks=sink.astype(F32) * LOG2E, use_sink=True, emit_lse=False), []


def _mixer_b(x, g, sc, sh, wqkv, rel_bias):
    nq = B_HEADS * HEAD_DIM
    w = jnp.concatenate([_scaled_q_weights(wqkv[:, :nq]), wqkv[:, nq:].astype(BF16)], axis=1)
    qkv = _qkv_call(x, g, sc, sh, w)[:, 0]
    return [_nbr_call(qkv, _nbr_bias_table(rel_bias))], []


def _mixer_c(x, g, sc, sh, wqkv, lam, subln, lam_init):
    nq = 2 * C_HEADS * HEAD_DIM
    w = jnp.concatenate([_scaled_q_weights(wqkv[:, :nq]), wqkv[:, nq:2 * nq].astype(BF16)], axis=1)
    qk, vt = _qkv_call(x, g, sc, sh, w, wqkv[:, 2 * nq:].T.astype(BF16), tile=_diff_tile(x.shape[1]))
    qk = qk[:, 0]
    lf = lam.astype(F32)
    lam_full = jnp.exp(jnp.sum(lf[0] * lf[1])) - jnp.exp(jnp.sum(lf[2] * lf[3])) + lam_init
    return [_diff_call(qk, vt, _alibi_slopes(C_HEADS) * LOG2E, lam_full.reshape(1), subln,
                       1.0 - lam_init)], []


def _mixer_d(x, g, sc, sh, wqkv):
    d = x.shape[-1]
    nh = D_HEADS * HEAD_DIM
    n_tiles = D_HEADS // 2
    outs, lses = [], []
    for gi, (win, dil) in enumerate(D_GROUPS):
        wg = wqkv[:, gi * 3 * nh:(gi + 1) * 3 * nh]
        w = jnp.concatenate([_scaled_q_weights(wg[:, :nh]), wg[:, nh:].astype(BF16)], axis=1)
        qkv = _qkv_call(x, g, sc, sh, w, dil=dil)
        o, lse = _banded_call(
            qkv, q_blk=lambda p: p, k_blk=lambda p: n_tiles + p, v_blk=lambda p: 2 * n_tiles + p,
            half=win // (2 * dil), dil=dil, n_groups=n_tiles, n_u=1,
            slopes=_alibi_slopes(D_HEADS) * (LOG2E * dil), sinks=jnp.zeros((D_HEADS,), F32),
            use_sink=False, emit_lse=True)
        outs.append(o)
        lses.append(lse)
    return outs, lses


def _trunk(x, mod, g_mix_pre, g_mix_post, g_mlp_pre, g_mlp_post, mlp_w1, mlp_w2,
           a_wqkv, a_wo, a_sink, b_wqkv, b_wo, b_rel_bias, c_wqkv, c_wo, c_lambda, c_subln, d_wqkv, d_wo):
    depth = mod.shape[0]
    d = x.shape[-1]
    for i in range(depth):
        sh1, sc1, gt1, sh2, sc2, gt2 = [mod[i, :, None, k * d:(k + 1) * d] for k in range(6)]
        kind, occ = i % N_MIXERS, i // N_MIXERS
        if kind == 0:
            (o, lse), wo = _mixer_a(x, g_mix_pre[i], sc1, sh1, a_wqkv[occ], a_sink[occ]), a_wo[occ]
        elif kind == 1:
            (o, lse), wo = _mixer_b(x, g_mix_pre[i], sc1, sh1, b_wqkv[occ], b_rel_bias[occ]), b_wo[occ]
        elif kind == 2:
            (o, lse), wo = _mixer_c(x, g_mix_pre[i], sc1, sh1, c_wqkv[occ], c_lambda[occ], c_subln[occ],
                                    0.8 - 0.6 * math.exp(-0.3 * i)), c_wo[occ]
        else:
            (o, lse), wo = _mixer_d(x, g_mix_pre[i], sc1, sh1, d_wqkv[occ]), d_wo[occ]
        x = _tail_call(o, lse, wo.astype(BF16), x, gt1, g_mix_post[i], g_mlp_pre[i], sc2, sh2,
                       mlp_w1[i].astype(BF16), mlp_w2[i].astype(BF16), gt2, g_mlp_post[i])
    return x


def kernel(x_prompt, x_sample, c_prompt, c_sample, ada_w, ada_b, g_mix_pre, g_mix_post, g_mlp_pre, g_mlp_post, mlp_w1, mlp_w2, a_wqkv, a_wo, a_sink, b_wqkv, b_wo, b_rel_bias, c_wqkv, c_wo, c_lambda, c_subln, d_wqkv, d_wo):
    bp, bs = c_prompt.shape[0], c_sample.shape[0]
    c_all = jnp.concatenate([c_prompt, c_sample], axis=0)
    pad = (-c_all.shape[0]) % 8
    c_all = jnp.pad(c_all, ((0, pad), (0, 0)))
    mod = _adaln_call(c_all, ada_w, ada_b)
    params = (g_mix_pre, g_mix_post, g_mlp_pre, g_mlp_post, mlp_w1, mlp_w2, a_wqkv, a_wo, a_sink,
              b_wqkv, b_wo, b_rel_bias, c_wqkv, c_wo, c_lambda, c_subln, d_wqkv, d_wo)
    y_prompt = _trunk(x_prompt, mod[:, :bp], *params)
    y_sample = _trunk(x_sample, mod[:, bp:bp + bs], *params)
    return (y_prompt, y_sample)
```

```python
import functools
import math

import jax
import jax.numpy as jnp
from jax import lax
from jax.experimental import pallas as pl
from jax.experimental.pallas import tpu as pltpu

F32 = jnp.float32
BF16 = jnp.bfloat16

D_MODEL = 1024
HEAD_DIM = 64
LANES = 128
BF16_ROWS = 16
D_FF = 4 * D_MODEL
EPS = 1e-6
NEG = -1e30
LOG2E = math.log2(math.e)
Q_SCALE = HEAD_DIM ** -0.5 * LOG2E
N_MIXERS = 4

A_HEADS, A_KV_HEADS, A_WINDOW = 16, 4, 128
B_HEADS, GRID_W, WIN_R, WIN_C = 16, 64, 8, 16
C_HEADS = 8
D_GROUPS = ((128, 1), (512, 4), (2048, 16))
D_HEADS = 8

VMEM_LIMIT_BYTES = 56 * 1024 * 1024

_NT = (((1,), (1,)), ((), ()))


def _cparams(*sem):
    return pltpu.CompilerParams(dimension_semantics=sem, vmem_limit_bytes=VMEM_LIMIT_BYTES)


def _smem_spec():
    return pl.BlockSpec(memory_space=pltpu.SMEM)


def _alibi_slopes(n):
    return 2.0 ** (-8.0 * jnp.arange(1, n + 1, dtype=F32) / n)


def _modulated_norm(x, g, sc, sh):
    ms = jnp.mean(x * x, axis=-1, keepdims=True)
    return (x * lax.rsqrt(ms + EPS) * g) * (1.0 + sc) + sh


def _gated_norm_residual(x, h, g, gt):
    ms = jnp.mean(h * h, axis=-1, keepdims=True)
    return x + gt * (h * lax.rsqrt(ms + EPS) * g)


def _head_select(q, lane, hh):
    keep = (lane >= HEAD_DIM) if hh else (lane < HEAD_DIM)
    return jnp.where(keep, q, jnp.zeros_like(q))


def _scaled_q_weights(wq):
    return (wq * Q_SCALE).astype(BF16)


def _adaln_kernel(c_ref, w_ref, b_ref, o_ref):
    c = c_ref[...]
    ca = (c * jax.nn.sigmoid(c)).astype(BF16)
    o_ref[0] = jnp.dot(ca, w_ref[0].astype(BF16), preferred_element_type=F32) + b_ref[0]


def _adaln_call(c_pad, ada_w, ada_b):
    depth, d, n = ada_w.shape
    bp = c_pad.shape[0]
    tn = 1536
    return pl.pallas_call(
        _adaln_kernel,
        grid=(depth, n // tn),
        in_specs=[
            pl.BlockSpec((bp, d), lambda l, j: (0, 0)),
            pl.BlockSpec((1, d, tn), lambda l, j: (l, 0, j)),
            pl.BlockSpec((1, 1, tn), lambda l, j: (l, 0, j)),
        ],
        out_specs=pl.BlockSpec((1, bp, tn), lambda l, j: (l, 0, j)),
        out_shape=jax.ShapeDtypeStruct((depth, bp, n), F32),
        compiler_params=_cparams("parallel", "parallel"),
        name="adaln",
    )(c_pad, ada_w, ada_b.reshape(depth, 1, n))


def _qkv_kernel(*refs, n_chunk, has_t, dil):
    if has_t:
        x_ref, g_ref, sc_ref, sh_ref, w_ref, wt_ref, o_ref, ot_ref = refs
    elif dil > 1:
        x_ref, g_ref, sc_ref, sh_ref, w_ref, o_ref, slab_s, h_s = refs
    else:
        x_ref, g_ref, sc_ref, sh_ref, w_ref, o_ref = refs
    tm = o_ref.shape[2]
    hn = _modulated_norm(x_ref[0], g_ref[...], sc_ref[0], sh_ref[0])
    if dil == 1:
        h = hn.astype(BF16)
    else:
        for c in range(slab_s.shape[0]):
            slab_s[c] = hn[:, c * LANES:(c + 1) * LANES]
        for rho in range(dil):
            for c in range(slab_s.shape[0]):
                h_s[rho * tm:(rho + 1) * tm, c * LANES:(c + 1) * LANES] = (
                    slab_s[c, pl.ds(rho, tm, stride=dil), :].astype(BF16))
        h = h_s[...]
    n = w_ref.shape[1]
    for c0 in range(0, n, n_chunk):
        oc = jnp.dot(h, w_ref[:, c0:c0 + n_chunk], preferred_element_type=F32).astype(o_ref.dtype)
        o_ref[0, :, :, c0:c0 + n_chunk] = oc.reshape(dil, tm, n_chunk)
    if has_t:
        ht = lax.dot_general(wt_ref[...], h, _NT, preferred_element_type=F32).astype(ot_ref.dtype)
        ot_ref[0, :, 0] = ht.reshape(ot_ref.shape[1], LANES, ht.shape[1])


def _qkv_call(x, g, sc, sh, w, wt=None, dil=1, tile=512):
    b, s, d = x.shape
    n = w.shape[1]
    sd = s // dil
    tile = min(tile, s)
    tm = tile // dil
    assert tm % BF16_ROWS == 0
    in_specs = [
        pl.BlockSpec((1, tile, d), lambda bi, i: (bi, i, 0)),
        pl.BlockSpec((1, d), lambda bi, i: (0, 0)),
        pl.BlockSpec((1, 1, d), lambda bi, i: (bi, 0, 0)),
        pl.BlockSpec((1, 1, d), lambda bi, i: (bi, 0, 0)),
        pl.BlockSpec((d, n), lambda bi, i: (0, 0)),
    ]
    out_shape = [jax.ShapeDtypeStruct((b, dil, sd, n), BF16)]
    out_specs = [pl.BlockSpec((1, dil, tm, n), lambda bi, i: (bi, 0, i, 0))]
    args = [x, g.reshape(1, d), sc, sh, w]
    scratch = []
    if wt is not None:
        assert dil == 1
        nt = wt.shape[0]
        in_specs.append(pl.BlockSpec((nt, d), lambda bi, i: (0, 0)))
        out_shape.append(jax.ShapeDtypeStruct((b, nt // LANES, s // tile, LANES, tile), BF16))
        out_specs.append(pl.BlockSpec((1, nt // LANES, 1, LANES, tile), lambda bi, i: (bi, 0, i, 0, 0)))
        args.append(wt)
    if dil > 1:
        scratch += [pltpu.VMEM((d // LANES, tile, LANES), F32), pltpu.VMEM((tile, d), BF16)]
    outs = pl.pallas_call(
        functools.partial(_qkv_kernel, n_chunk=512, has_t=wt is not None, dil=dil),
        grid=(b, s // tile),
        in_specs=in_specs,
        out_specs=out_specs,
        out_shape=out_shape,
        scratch_shapes=scratch,
        compiler_params=_cparams("parallel", "parallel"),
        name="qkv_proj",
    )(*args)
    return (outs[0], outs[1]) if wt is not None else outs[0]


def _merge_groups(o_refs, lse_refs):
    m = lse_refs[0][0]
    for r in lse_refs[1:]:
        m = jnp.maximum(m, r[0])
    num = jnp.zeros(m.shape, F32)
    den = jnp.zeros(m.shape, F32)
    for o_r, l_r in zip(o_refs, lse_refs):
        e = jnp.exp2(l_r[0] - m)
        den = den + e
        num = num + e * o_r[0]
    return num / den


def _tail_kernel(*refs, n_mix, f_chunk):
    n_o = max(n_mix, 1)
    o_refs, lse_refs = refs[:n_o], refs[n_o:n_o + n_mix]
    (wo_ref, x_ref, gt1_ref, g1_ref, g2_ref, sc_ref, sh_ref, w1_ref, w2_ref, gt2_ref, g3_ref,
     y_ref) = refs[n_o + n_mix:]
    o = _merge_groups(o_refs, lse_refs).astype(BF16) if n_mix else o_refs[0][0]
    x1 = _gated_norm_residual(x_ref[0], jnp.dot(o, wo_ref[...], preferred_element_type=F32),
                              g1_ref[...], gt1_ref[0])
    h = _modulated_norm(x1, g2_ref[...], sc_ref[0], sh_ref[0]).astype(BF16)
    acc = jnp.zeros(x1.shape, F32)
    for c0 in range(0, w1_ref.shape[1], f_chunk):
        a = jnp.dot(h, w1_ref[:, c0:c0 + f_chunk], preferred_element_type=F32)
        a = jnp.maximum(a, 0.0)
        a = (a * a).astype(BF16)
        acc = acc + jnp.dot(a, w2_ref[c0:c0 + f_chunk, :], preferred_element_type=F32)
    y_ref[0] = _gated_norm_residual(x1, acc, g3_ref[...], gt2_ref[0])


def _tail_call(o_list, lse_list, wo, x, gt1, g1, g2, sc2, sh2, w1, w2, gt2, g3):
    b, s, d = x.shape
    kd = wo.shape[0]
    f = w1.shape[1]
    n_mix = len(lse_list)
    tm = min(256 if n_mix else 512, s)
    tile = pl.BlockSpec((1, tm, kd), lambda bi, i: (bi, i, 0))
    row = pl.BlockSpec((1, tm, d), lambda bi, i: (bi, i, 0))
    vec = pl.BlockSpec((1, 1, d), lambda bi, i: (bi, 0, 0))
    gain = pl.BlockSpec((1, d), lambda bi, i: (0, 0))

    def resident(shape):
        return pl.BlockSpec(shape, lambda bi, i: (0, 0), pipeline_mode=pl.Buffered(1))

    return pl.pallas_call(
        functools.partial(_tail_kernel, n_mix=n_mix, f_chunk=1024),
        grid=(b, s // tm),
        in_specs=[tile] * (len(o_list) + n_mix) + [
            resident((kd, d)), row, vec, gain, gain, vec, vec, resident((d, f)), resident((f, d)), vec, gain],
        out_specs=row,
        out_shape=jax.ShapeDtypeStruct((b, s, d), F32),
        compiler_params=_cparams("parallel", "parallel"),
        name="layer_tail",
    )(*o_list, *lse_list, wo, x, gt1, g1.reshape(1, d), g2.reshape(1, d), sc2, sh2, w1, w2, gt2,
      g3.reshape(1, d))


def _banded_kernel(slopes_ref, sink_ref, q_ref, k_ref, v_ref, tab_ref, *out_refs,
                   half, tq_sub, n_sub, n_u, dil, seq, use_sink, emit_lse):
    o_ref = out_refs[0]
    n_heads = 2 * n_u
    head0 = pl.program_id(1) * n_heads
    span = tq_sub + 2 * half
    lane = lax.broadcasted_iota(jnp.int32, (tq_sub, LANES), 1)
    base = (lax.broadcasted_iota(jnp.int32, (tq_sub, span), 1)
            - lax.broadcasted_iota(jnp.int32, (tq_sub, span), 0) - half)
    def block(sb):
        if dil == 1:
            r, q0 = 0, (pl.program_id(2) * n_sub + sb) * tq_sub
            qrows = slice(sb * tq_sub, (sb + 1) * tq_sub)
            rows = qrows
        else:
            r, q0 = sb, pl.program_id(2) * tq_sub
            qrows = slice(0, tq_sub)
            rows = pl.ds(r, tq_sub, stride=dil)
        start = pl.multiple_of(jnp.clip(q0 - half, 0, seq - span), HEAD_DIM)
        return r, qrows, rows, q0, start

    def scores(sb):
        r, qrows, _, _, start = block(sb)
        qs = []
        for u in range(n_u):
            qu = q_ref[0, r, qrows, u * LANES:(u + 1) * LANES]
            qs += [_head_select(qu, lane, 0), _head_select(qu, lane, 1)]
        return lax.dot_general(jnp.concatenate(qs, axis=0), k_ref[0, r, pl.ds(start, span), :], _NT,
                               preferred_element_type=F32)

    def finish(sb, s_all, unclamped):
        r, _, rows, q0, start = block(sb)
        vw = v_ref[0, r, pl.ds(start, span), :]
        if not unclamped:
            dist = jnp.abs(base + (start - q0 + half))
            valid = dist <= half
            distf = dist.astype(F32)
        prs, dens, lses = [], [], []
        for hd in range(n_heads):
            s = s_all[hd * tq_sub:(hd + 1) * tq_sub]
            if unclamped:
                s = s + tab_ref[hd]
            else:
                s = jnp.where(valid, s - slopes_ref[head0 + hd] * distf, NEG)
            m = jnp.max(s, axis=-1, keepdims=True)
            if use_sink:
                sink = sink_ref[head0 + hd]
                m = jnp.maximum(m, sink)
            pr = jnp.exp2(s - m)
            den = jnp.sum(pr, axis=-1, keepdims=True)
            if use_sink:
                den = den + jnp.exp2(sink - m)
            prs.append(pr.astype(BF16))
            dens.append(den)
            lses.append(m + jnp.log2(den))
        acc_all = jnp.dot(jnp.concatenate(prs, axis=0), vw, preferred_element_type=F32)
        for u in range(n_u):
            r0 = acc_all[(2 * u) * tq_sub:(2 * u + 1) * tq_sub] / dens[2 * u]
            r1 = acc_all[(2 * u + 1) * tq_sub:(2 * u + 2) * tq_sub] / dens[2 * u + 1]
            cols = slice(u * LANES, (u + 1) * LANES)
            o_ref[0, rows, cols] = jnp.where(lane < HEAD_DIM, r0, r1).astype(o_ref.dtype)
            if emit_lse:
                out_refs[1][0, rows, cols] = jnp.where(lane < HEAD_DIM, lses[2 * u], lses[2 * u + 1])

    def run(unclamped):
        ahead = min(2, n_sub)
        pending = [scores(sb) for sb in range(ahead)]
        for sb in range(n_sub):
            if sb + ahead < n_sub:
                pending.append(scores(sb + ahead))
            finish(sb, pending.pop(0), unclamped)

    first_q0 = block(0)[3]
    last_q0 = block(n_sub - 1)[3]
    unclamped = (first_q0 >= half) & (last_q0 + tq_sub + half <= seq)
    pl.when(unclamped)(functools.partial(run, True))
    pl.when(jnp.logical_not(unclamped))(functools.partial(run, False))


def _banded_bias_table(slopes, half, tq_sub):
    span = tq_sub + 2 * half
    dist = jnp.abs(jnp.arange(span)[None, :] - jnp.arange(tq_sub)[:, None] - half)
    return jnp.where(dist <= half, -slopes[:, None, None] * dist.astype(F32), NEG)


def _banded_call(qkv, q_blk, k_blk, v_blk, half, dil, n_groups, n_u, slopes, sinks, use_sink, emit_lse):
    b, _, sd, _ = qkv.shape
    tq_sub = 128
    n_sub = min(4, sd // tq_sub) if dil == 1 else dil
    tq_stream = tq_sub * n_sub // dil
    wq = n_u * LANES
    assert dil == 1 or (n_u == 1 and emit_lse)
    out_dtype = F32 if emit_lse else BF16
    out_spec = pl.BlockSpec((1, tq_sub * n_sub, wq), lambda bi, p, i: (bi, i, p))
    out_shape = [jax.ShapeDtypeStruct((b, sd * dil, n_groups * wq), out_dtype)]
    out_specs = [out_spec]
    if emit_lse:
        out_shape.append(out_shape[0])
        out_specs.append(out_spec)
    return pl.pallas_call(
        functools.partial(_banded_kernel, half=half, tq_sub=tq_sub, n_sub=n_sub, n_u=n_u, dil=dil,
                          seq=sd, use_sink=use_sink, emit_lse=emit_lse),
        grid=(b, n_groups, sd // tq_stream),
        in_specs=[
            _smem_spec(), _smem_spec(),
            pl.BlockSpec((1, dil, tq_stream, wq), lambda bi, p, i: (bi, 0, i, q_blk(p))),
            pl.BlockSpec((1, dil, sd, LANES), lambda bi, p, i: (bi, 0, 0, k_blk(p))),
            pl.BlockSpec((1, dil, sd, LANES), lambda bi, p, i: (bi, 0, 0, v_blk(p))),
            pl.BlockSpec((2 * n_u, tq_sub, tq_sub + 2 * half), lambda bi, p, i: (p, 0, 0)),
        ],
        out_specs=out_specs,
        out_shape=out_shape,
        compiler_params=_cparams("parallel", "parallel", "arbitrary"),
        name="banded_attn",
    )(slopes, sinks, qkv, qkv, qkv, _banded_bias_table(slopes, half, tq_sub))


def _nbr_kernel(q_ref, k_ref, v_ref, t_ref, o_ref, *, rows_per_tile, n_rows):
    i = pl.program_id(2)
    nkeys = WIN_R * GRID_W
    lane = lax.broadcasted_iota(jnp.int32, (GRID_W, LANES), 1)
    def band(rr):
        r = i * rows_per_tile + rr
        rs = jnp.clip(r - WIN_R // 2, 0, n_rows - WIN_R)
        off = rs - r + (WIN_R - 1)
        return pl.multiple_of(rs * GRID_W, GRID_W), off

    def scores(rr):
        start, _ = band(rr)
        q = q_ref[0, rr * GRID_W:(rr + 1) * GRID_W, :]
        q2 = jnp.concatenate([_head_select(q, lane, 0), _head_select(q, lane, 1)], axis=0)
        return lax.dot_general(q2, k_ref[0, pl.ds(start, nkeys), :], _NT, preferred_element_type=F32)

    def finish(rr, s_all):
        start, off = band(rr)
        prs, dens = [], []
        for hh in range(2):
            s = s_all[hh * GRID_W:(hh + 1) * GRID_W] + t_ref[0, hh, off]
            m = jnp.max(s, axis=-1, keepdims=True)
            pr = jnp.exp2(s - m)
            dens.append(jnp.sum(pr, axis=-1, keepdims=True))
            prs.append(pr.astype(BF16))
        acc = jnp.dot(jnp.concatenate(prs, axis=0), v_ref[0, pl.ds(start, nkeys), :],
                      preferred_element_type=F32)
        o_ref[0, rr * GRID_W:(rr + 1) * GRID_W, :] = jnp.where(
            lane < HEAD_DIM, acc[:GRID_W] / dens[0], acc[GRID_W:] / dens[1]).astype(o_ref.dtype)

    ahead = min(2, rows_per_tile)
    pending = [scores(rr) for rr in range(ahead)]
    for rr in range(rows_per_tile):
        if rr + ahead < rows_per_tile:
            pending.append(scores(rr + ahead))
        finish(rr, pending.pop(0))


def _nbr_bias_table(rel_bias):
    col = jnp.arange(GRID_W)
    cs = jnp.clip(col - WIN_C // 2, 0, GRID_W - WIN_C)
    col_valid = (col[None, :] >= cs[:, None]) & (col[None, :] < cs[:, None] + WIN_C)
    dc_idx = jnp.clip(col[None, :] - col[:, None] + WIN_C - 1, 0, 2 * WIN_C - 2)
    dr = jnp.arange(WIN_R)[:, None] + jnp.arange(WIN_R)[None, :]
    tab = rel_bias.astype(F32)[:, dr] * LOG2E
    tab = tab[..., dc_idx]
    tab = jnp.where(col_valid, tab, NEG)
    tab = jnp.transpose(tab, (0, 1, 3, 2, 4))
    h = tab.shape[0]
    return tab.reshape(h // 2, 2, WIN_R, GRID_W, WIN_R * GRID_W)


def _nbr_call(qkv, table):
    b, s, _ = qkv.shape
    n_tiles = B_HEADS // 2
    n_rows = s // GRID_W
    rows_per_tile = min(16, n_rows)
    tq = rows_per_tile * GRID_W
    kv = lambda off: pl.BlockSpec((1, s, LANES), lambda bi, p, i: (bi, 0, off + p))
    return pl.pallas_call(
        functools.partial(_nbr_kernel, rows_per_tile=rows_per_tile, n_rows=n_rows),
        grid=(b, n_tiles, s // tq),
        in_specs=[
            pl.BlockSpec((1, tq, LANES), lambda bi, p, i: (bi, i, p)),
            kv(n_tiles), kv(2 * n_tiles),
            pl.BlockSpec((1, 2, WIN_R, GRID_W, WIN_R * GRID_W), lambda bi, p, i: (p, 0, 0, 0, 0)),
        ],
        out_specs=pl.BlockSpec((1, tq, LANES), lambda bi, p, i: (bi, i, p)),
        out_shape=jax.ShapeDtypeStruct((b, s, n_tiles * LANES), BF16),
        compiler_params=_cparams("parallel", "parallel", "arbitrary"),
        name="nbr_attn",
    )(qkv, qkv, qkv, table)


_POS_SPLIT = 32


def _diff_kernel(slopes_ref, lam_ref, q_ref, k_ref, vt_ref, kp_ref, g_ref, o_ref,
                 qa_s, m_s, acc_s, s_s, p_s, al_s, rb_s, mx_s, *, t, n_tiles, post_scale):
    h = pl.program_id(1)
    i = pl.program_id(2)
    slope = slopes_ref[h]
    ones_rows = jnp.ones((BF16_ROWS, t), BF16)

    q = q_ref[0]
    lane = lax.broadcasted_iota(jnp.int32, (t, LANES), 1)
    c = jnp.full((t, LANES), -slope, F32)
    c_hi = c.astype(BF16).astype(F32)
    aug = jnp.where(lane < 2, c_hi, jnp.where(lane < 4, c - c_hi, 0.0))
    for mp in range(2):
        qa = jnp.concatenate([_head_select(q, lane, mp).astype(F32), aug], axis=1)
        qa_s[mp] = qa.T.astype(BF16)

    def tile_of(pos):
        if pos == 0:
            return i, None
        after = (pos - 1) >= i
        return pos - 1 + jnp.where(after, 1, 0), after

    m_s[...] = jnp.full(m_s.shape, NEG, F32)
    acc_s[...] = jnp.zeros(acc_s.shape, F32)

    def stage_scores(pos, slot):
        j, after = tile_of(pos)
        kt = k_ref[0, pl.ds(pl.multiple_of(j * t, t), t), :]
        if pos == 0:
            kpos = lax.broadcasted_iota(jnp.int32, (t, t), 0)
            qpos = lax.broadcasted_iota(jnp.int32, (t, t), 1)
            bias = slope * jnp.abs(kpos - qpos).astype(F32)
            rb_s[slot] = jnp.zeros((1, t), F32)
        else:
            ka = jnp.concatenate([kt, kp_ref[jnp.where(after, 0, 1)]], axis=1)
            sgn = jnp.where(after, 1.0, -1.0).astype(F32)
            qrow = lax.broadcasted_iota(jnp.int32, (1, t), 1).astype(F32)
            blk = (jnp.abs(j - i) * t).astype(F32)
            rb_s[slot] = slope * (sgn * qrow - blk)
        for mp in range(2):
            if pos == 0:
                s = jnp.dot(kt, qa_s[mp, :LANES, :], preferred_element_type=F32) - bias
            else:
                s = jnp.dot(ka, qa_s[mp], preferred_element_type=F32)
            s_s[slot, mp] = s
            mx_s[slot, mp] = jnp.max(s, axis=0, keepdims=True)

    def stage_softmax(slot):
        rb = rb_s[slot]
        for mp in range(2):
            m_old = m_s[mp]
            m_new = jnp.maximum(m_old, mx_s[slot, mp] + rb)
            al_s[slot, mp] = jnp.exp2(m_old - m_new)
            p_s[slot, mp] = jnp.exp2(s_s[slot, mp] - (m_new - rb)).astype(BF16)
            m_s[mp] = m_new

    def stage_values(pos, slot):
        j, _ = tile_of(pos)
        va = jnp.concatenate([vt_ref[0, 0, j], ones_rows], axis=0)
        for mp in range(2):
            acc_s[mp] = al_s[slot, mp] * acc_s[mp] + jnp.dot(va, p_s[slot, mp], preferred_element_type=F32)

    for n in range(n_tiles + 2):
        if 2 <= n:
            stage_values(n - 2, n % 2)
        if 1 <= n <= n_tiles:
            stage_softmax(1 - n % 2)
        if n < n_tiles:
            stage_scores(n, n % 2)

    o0 = acc_s[0, :LANES] / acc_s[0, LANES:LANES + 1]
    o1 = acc_s[1, :LANES] / acc_s[1, LANES:LANES + 1]
    o = o0 - lam_ref[0] * o1
    ms = jnp.mean(o * o, axis=0, keepdims=True)
    y = (o * lax.rsqrt(ms + EPS) * g_ref[...]) * post_scale
    o_ref[0] = y.T.astype(o_ref.dtype)


def _key_offset_tiles(t):
    r = jnp.arange(t)
    hi = (r // _POS_SPLIT * _POS_SPLIT).astype(F32)
    lo = (r % _POS_SPLIT).astype(F32)
    tile = jnp.zeros((t, LANES), F32)
    tile = tile.at[:, 0].set(hi).at[:, 1].set(lo).at[:, 2].set(hi).at[:, 3].set(lo)
    return jnp.stack([tile, -tile]).astype(BF16)


def _diff_tile(s):
    return min(512, s)


def _diff_call(qk, vt, slopes, lam, subln, post_scale):
    b, s, _ = qk.shape
    t = _diff_tile(s)
    n_tiles = s // t
    assert t <= _POS_SPLIT * BF16_ROWS
    return pl.pallas_call(
        functools.partial(_diff_kernel, t=t, n_tiles=n_tiles, post_scale=post_scale),
        grid=(b, C_HEADS, n_tiles),
        in_specs=[
            _smem_spec(), _smem_spec(),
            pl.BlockSpec((1, t, LANES), lambda bi, h, i: (bi, i, h)),
            pl.BlockSpec((1, s, LANES), lambda bi, h, i: (bi, 0, C_HEADS + h)),
            pl.BlockSpec((1, 1, n_tiles, LANES, t), lambda bi, h, i: (bi, h, 0, 0, 0)),
            pl.BlockSpec((2, t, LANES), lambda bi, h, i: (0, 0, 0)),
            pl.BlockSpec((LANES, 1), lambda bi, h, i: (0, 0)),
        ],
        out_specs=pl.BlockSpec((1, t, LANES), lambda bi, h, i: (bi, i, h)),
        out_shape=jax.ShapeDtypeStruct((b, s, C_HEADS * LANES), BF16),
        scratch_shapes=[
            pltpu.VMEM((2, 2 * LANES, t), BF16),
            pltpu.VMEM((2, 1, t), F32),
            pltpu.VMEM((2, LANES + BF16_ROWS, t), F32),
            pltpu.VMEM((2, 2, t, t), F32),
            pltpu.VMEM((2, 2, t, t), BF16),
            pltpu.VMEM((2, 2, 1, t), F32),
            pltpu.VMEM((2, 1, t), F32),
            pltpu.VMEM((2, 2, 1, t), F32),
        ],
        compiler_params=_cparams("parallel", "parallel", "parallel"),
        name="diff_attn",
    )(slopes, lam, qk, qk, vt, _key_offset_tiles(t), subln.astype(F32).reshape(LANES, 1))


def _mixer_a(x, g, sc, sh, wqkv, sink):
    d = x.shape[-1]
    nq = A_HEADS * HEAD_DIM
    nkv = A_KV_HEADS * HEAD_DIM

    def dup(wpart):
        wpart = wpart.reshape(d, A_KV_HEADS, 1, HEAD_DIM)
        return jnp.broadcast_to(wpart, (d, A_KV_HEADS, 2, HEAD_DIM)).reshape(d, 2 * nkv).astype(BF16)

    w = jnp.concatenate([_scaled_q_weights(wqkv[:, :nq]), dup(wqkv[:, nq:nq + nkv]),
                         dup(wqkv[:, nq + nkv:])], axis=1)
    qkv = _qkv_call(x, g, sc, sh, w)
    q_tiles = nq // LANES
    kv_tiles = 2 * nkv // LANES
    pairs_per_kv = A_HEADS // A_KV_HEADS // 2
    return _banded_call(
        qkv, q_blk=lambda kv: kv, k_blk=lambda kv: q_tiles + kv, v_blk=lambda kv: q_tiles + kv_tiles + kv,
        half=A_WINDOW, dil=1, n_groups=A_KV_HEADS, n_u=pairs_per_kv, slopes=_alibi_slopes(A_HEADS) * LOG2E,
        sinks=sink.astype(F32) * LOG2E, use_sink=True, emit_lse=False), []


def _mixer_b(x, g, sc, sh, wqkv, rel_bias):
    nq = B_HEADS * HEAD_DIM
    w = jnp.concatenate([_scaled_q_weights(wqkv[:, :nq]), wqkv[:, nq:].astype(BF16)], axis=1)
    qkv = _qkv_call(x, g, sc, sh, w)[:, 0]
    return [_nbr_call(qkv, _nbr_bias_table(rel_bias))], []


def _mixer_c(x, g, sc, sh, wqkv, lam, subln, lam_init):
    nq = 2 * C_HEADS * HEAD_DIM
    w = jnp.concatenate([_scaled_q_weights(wqkv[:, :nq]), wqkv[:, nq:2 * nq].astype(BF16)], axis=1)
    qk, vt = _qkv_call(x, g, sc, sh, w, wqkv[:, 2 * nq:].T.astype(BF16), tile=_diff_tile(x.shape[1]))
    qk = qk[:, 0]
    lf = lam.astype(F32)
    lam_full = jnp.exp(jnp.sum(lf[0] * lf[1])) - jnp.exp(jnp.sum(lf[2] * lf[3])) + lam_init
    return [_diff_call(qk, vt, _alibi_slopes(C_HEADS) * LOG2E, lam_full.reshape(1), subln,
                       1.0 - lam_init)], []


def _mixer_d(x, g, sc, sh, wqkv):
    d = x.shape[-1]
    nh = D_HEADS * HEAD_DIM
    n_tiles = D_HEADS // 2
    outs, lses = [], []
    for gi, (win, dil) in enumerate(D_GROUPS):
        wg = wqkv[:, gi * 3 * nh:(gi + 1) * 3 * nh]
        w = jnp.concatenate([_scaled_q_weights(wg[:, :nh]), wg[:, nh:].astype(BF16)], axis=1)
        qkv = _qkv_call(x, g, sc, sh, w, dil=dil)
        o, lse = _banded_call(
            qkv, q_blk=lambda p: p, k_blk=lambda p: n_tiles + p, v_blk=lambda p: 2 * n_tiles + p,
            half=win // (2 * dil), dil=dil, n_groups=n_tiles, n_u=1,
            slopes=_alibi_slopes(D_HEADS) * (LOG2E * dil), sinks=jnp.zeros((D_HEADS,), F32),
            use_sink=False, emit_lse=True)
        outs.append(o)
        lses.append(lse)
    return outs, lses


def _trunk(x, mod, g_mix_pre, g_mix_post, g_mlp_pre, g_mlp_post, mlp_w1, mlp_w2,
           a_wqkv, a_wo, a_sink, b_wqkv, b_wo, b_rel_bias, c_wqkv, c_wo, c_lambda, c_subln, d_wqkv, d_wo):
    depth = mod.shape[0]
    d = x.shape[-1]
    for i in range(depth):
        sh1, sc1, gt1, sh2, sc2, gt2 = [mod[i, :, None, k * d:(k + 1) * d] for k in range(6)]
        kind, occ = i % N_MIXERS, i // N_MIXERS
        if kind == 0:
            (o, lse), wo = _mixer_a(x, g_mix_pre[i], sc1, sh1, a_wqkv[occ], a_sink[occ]), a_wo[occ]
        elif kind == 1:
            (o, lse), wo = _mixer_b(x, g_mix_pre[i], sc1, sh1, b_wqkv[occ], b_rel_bias[occ]), b_wo[occ]
        elif kind == 2:
            (o, lse), wo = _mixer_c(x, g_mix_pre[i], sc1, sh1, c_wqkv[occ], c_lambda[occ], c_subln[occ],
                                    0.8 - 0.6 * math.exp(-0.3 * i)), c_wo[occ]
        else:
            (o, lse), wo = _mixer_d(x, g_mix_pre[i], sc1, sh1, d_wqkv[occ]), d_wo[occ]
        x = _tail_call(o, lse, wo.astype(BF16), x, gt1, g_mix_post[i], g_mlp_pre[i], sc2, sh2,
                       mlp_w1[i].astype(BF16), mlp_w2[i].astype(BF16), gt2, g_mlp_post[i])
    return x


def kernel(x_prompt, x_sample, c_prompt, c_sample, ada_w, ada_b, g_mix_pre, g_mix_post, g_mlp_pre, g_mlp_post, mlp_w1, mlp_w2, a_wqkv, a_wo, a_sink, b_wqkv, b_wo, b_rel_bias, c_wqkv, c_wo, c_lambda, c_subln, d_wqkv, d_wo):
    bp, bs = c_prompt.shape[0], c_sample.shape[0]
    c_all = jnp.concatenate([c_prompt, c_sample], axis=0)
    pad = (-c_all.shape[0]) % 8
    c_all = jnp.pad(c_all, ((0, pad), (0, 0)))
    mod = _adaln_call(c_all, ada_w, ada_b)
    params = (g_mix_pre, g_mix_post, g_mlp_pre, g_mlp_post, mlp_w1, mlp_w2, a_wqkv, a_wo, a_sink,
              b_wqkv, b_wo, b_rel_bias, c_wqkv, c_wo, c_lambda, c_subln, d_wqkv, d_wo)
    y_prompt = _trunk(x_prompt, mod[:, :bp], *params)
    y_sample = _trunk(x_sample, mod[:, bp:bp + bs], *params)
    return (y_prompt, y_sample)
```

```python
import functools
import math

import jax
import jax.numpy as jnp
from jax import lax
from jax.experimental import pallas as pl
from jax.experimental.pallas import tpu as pltpu

F32 = jnp.float32
BF16 = jnp.bfloat16

D_MODEL = 1024
HEAD_DIM = 64
LANES = 128
BF16_ROWS = 16
D_FF = 4 * D_MODEL
EPS = 1e-6
NEG = -1e30
LOG2E = math.log2(math.e)
Q_SCALE = HEAD_DIM ** -0.5 * LOG2E
N_MIXERS = 4

A_HEADS, A_KV_HEADS, A_WINDOW = 16, 4, 128
B_HEADS, GRID_W, WIN_R, WIN_C = 16, 64, 8, 16
C_HEADS = 8
D_GROUPS = ((128, 1), (512, 4), (2048, 16))
D_HEADS = 8

VMEM_LIMIT_BYTES = 56 * 1024 * 1024

_NT = (((1,), (1,)), ((), ()))


def _cparams(*sem):
    return pltpu.CompilerParams(dimension_semantics=sem, vmem_limit_bytes=VMEM_LIMIT_BYTES)


def _smem_spec():
    return pl.BlockSpec(memory_space=pltpu.SMEM)


def _alibi_slopes(n):
    return 2.0 ** (-8.0 * jnp.arange(1, n + 1, dtype=F32) / n)


def _modulated_norm(x, g, sc, sh):
    ms = jnp.mean(x * x, axis=-1, keepdims=True)
    return (x * lax.rsqrt(ms + EPS) * g) * (1.0 + sc) + sh


def _gated_norm_residual(x, h, g, gt):
    ms = jnp.mean(h * h, axis=-1, keepdims=True)
    return x + gt * (h * lax.rsqrt(ms + EPS) * g)


def _head_select(q, lane, hh):
    keep = (lane >= HEAD_DIM) if hh else (lane < HEAD_DIM)
    return jnp.where(keep, q, jnp.zeros_like(q))


def _scaled_q_weights(wq):
    return (wq * Q_SCALE).astype(BF16)


def _adaln_kernel(c_ref, w_ref, b_ref, o_ref):
    c = c_ref[...]
    ca = (c * jax.nn.sigmoid(c)).astype(BF16)
    o_ref[0] = jnp.dot(ca, w_ref[0].astype(BF16), preferred_element_type=F32) + b_ref[0]


def _adaln_call(c_pad, ada_w, ada_b):
    depth, d, n = ada_w.shape
    bp = c_pad.shape[0]
    tn = 1536
    return pl.pallas_call(
        _adaln_kernel,
        grid=(depth, n // tn),
        in_specs=[
            pl.BlockSpec((bp, d), lambda l, j: (0, 0)),
            pl.BlockSpec((1, d, tn), lambda l, j: (l, 0, j)),
            pl.BlockSpec((1, 1, tn), lambda l, j: (l, 0, j)),
        ],
        out_specs=pl.BlockSpec((1, bp, tn), lambda l, j: (l, 0, j)),
        out_shape=jax.ShapeDtypeStruct((depth, bp, n), F32),
        compiler_params=_cparams("parallel", "parallel"),
        name="adaln",
    )(c_pad, ada_w, ada_b.reshape(depth, 1, n))


def _qkv_kernel(*refs, n_chunk, has_t, dil):
    if has_t:
        x_ref, g_ref, sc_ref, sh_ref, w_ref, wt_ref, o_ref, ot_ref = refs
    elif dil > 1:
        x_ref, g_ref, sc_ref, sh_ref, w_ref, o_ref, slab_s, h_s = refs
    else:
        x_ref, g_ref, sc_ref, sh_ref, w_ref, o_ref = refs
    tm = o_ref.shape[2]
    hn = _modulated_norm(x_ref[0], g_ref[...], sc_ref[0], sh_ref[0])
    if dil == 1:
        h = hn.astype(BF16)
    else:
        for c in range(slab_s.shape[0]):
            slab_s[c] = hn[:, c * LANES:(c + 1) * LANES]
        for rho in range(dil):
            for c in range(slab_s.shape[0]):
                h_s[rho * tm:(rho + 1) * tm, c * LANES:(c + 1) * LANES] = (
                    slab_s[c, pl.ds(rho, tm, stride=dil), :].astype(BF16))
        h = h_s[...]
    n = w_ref.shape[1]
    for c0 in range(0, n, n_chunk):
        oc = jnp.dot(h, w_ref[:, c0:c0 + n_chunk], preferred_element_type=F32).astype(o_ref.dtype)
        o_ref[0, :, :, c0:c0 + n_chunk] = oc.reshape(dil, tm, n_chunk)
    if has_t:
        ht = lax.dot_general(wt_ref[...], h, _NT, preferred_element_type=F32).astype(ot_ref.dtype)
        ot_ref[0, :, 0] = ht.reshape(ot_ref.shape[1], LANES, ht.shape[1])


def _qkv_call(x, g, sc, sh, w, wt=None, dil=1, tile=512):
    b, s, d = x.shape
    n = w.shape[1]
    sd = s // dil
    tile = min(tile, s)
    tm = tile // dil
    assert tm % BF16_ROWS == 0
    in_specs = [
        pl.BlockSpec((1, tile, d), lambda bi, i: (bi, i, 0)),
        pl.BlockSpec((1, d), lambda bi, i: (0, 0)),
        pl.BlockSpec((1, 1, d), lambda bi, i: (bi, 0, 0)),
        pl.BlockSpec((1, 1, d), lambda bi, i: (bi, 0, 0)),
        pl.BlockSpec((d, n), lambda bi, i: (0, 0)),
    ]
    out_shape = [jax.ShapeDtypeStruct((b, dil, sd, n), BF16)]
    out_specs = [pl.BlockSpec((1, dil, tm, n), lambda bi, i: (bi, 0, i, 0))]
    args = [x, g.reshape(1, d), sc, sh, w]
    scratch = []
    if wt is not None:
        assert dil == 1
        nt = wt.shape[0]
        in_specs.append(pl.BlockSpec((nt, d), lambda bi, i: (0, 0)))
        out_shape.append(jax.ShapeDtypeStruct((b, nt // LANES, s // tile, LANES, tile), BF16))
        out_specs.append(pl.BlockSpec((1, nt // LANES, 1, LANES, tile), lambda bi, i: (bi, 0, i, 0, 0)))
        args.append(wt)
    if dil > 1:
        scratch += [pltpu.VMEM((d // LANES, tile, LANES), F32), pltpu.VMEM((tile, d), BF16)]
    outs = pl.pallas_call(
        functools.partial(_qkv_kernel, n_chunk=512, has_t=wt is not None, dil=dil),
        grid=(b, s // tile),
        in_specs=in_specs,
        out_specs=out_specs,
        out_shape=out_shape,
        scratch_shapes=scratch,
        compiler_params=_cparams("parallel", "parallel"),
        name="qkv_proj",
    )(*args)
    return (outs[0], outs[1]) if wt is not None else outs[0]


def _merge_groups(o_refs, lse_refs):
    m = lse_refs[0][0]
    for r in lse_refs[1:]:
        m = jnp.maximum(m, r[0])
    num = jnp.zeros(m.shape, F32)
    den = jnp.zeros(m.shape, F32)
    for o_r, l_r in zip(o_refs, lse_refs):
        e = jnp.exp2(l_r[0] - m)
        den = den + e
        num = num + e * o_r[0]
    return num / den


def _tail_kernel(*refs, n_mix, f_chunk):
    n_o = max(n_mix, 1)
    o_refs, lse_refs = refs[:n_o], refs[n_o:n_o + n_mix]
    (wo_ref, x_ref, gt1_ref, g1_ref, g2_ref, sc_ref, sh_ref, w1_ref, w2_ref, gt2_ref, g3_ref,
     y_ref) = refs[n_o + n_mix:]
    o = _merge_groups(o_refs, lse_refs).astype(BF16) if n_mix else o_refs[0][0]
    x1 = _gated_norm_residual(x_ref[0], jnp.dot(o, wo_ref[...], preferred_element_type=F32),
                              g1_ref[...], gt1_ref[0])
    h = _modulated_norm(x1, g2_ref[...], sc_ref[0], sh_ref[0]).astype(BF16)
    acc = jnp.zeros(x1.shape, F32)
    for c0 in range(0, w1_ref.shape[1], f_chunk):
        a = jnp.dot(h, w1_ref[:, c0:c0 + f_chunk], preferred_element_type=F32)
        a = jnp.maximum(a, 0.0)
        a = (a * a).astype(BF16)
        acc = acc + jnp.dot(a, w2_ref[c0:c0 + f_chunk, :], preferred_element_type=F32)
    y_ref[0] = _gated_norm_residual(x1, acc, g3_ref[...], gt2_ref[0])


def _tail_call(o_list, lse_list, wo, x, gt1, g1, g2, sc2, sh2, w1, w2, gt2, g3):
    b, s, d = x.shape
    kd = wo.shape[0]
    f = w1.shape[1]
    n_mix = len(lse_list)
    tm = min(256 if n_mix else 512, s)
    tile = pl.BlockSpec((1, tm, kd), lambda bi, i: (bi, i, 0))
    row = pl.BlockSpec((1, tm, d), lambda bi, i: (bi, i, 0))
    vec = pl.BlockSpec((1, 1, d), lambda bi, i: (bi, 0, 0))
    gain = pl.BlockSpec((1, d), lambda bi, i: (0, 0))

    def resident(shape):
        return pl.BlockSpec(shape, lambda bi, i: (0, 0), pipeline_mode=pl.Buffered(1))

    return pl.pallas_call(
        functools.partial(_tail_kernel, n_mix=n_mix, f_chunk=1024),
        grid=(b, s // tm),
        in_specs=[tile] * (len(o_list) + n_mix) + [
            resident((kd, d)), row, vec, gain, gain, vec, vec, resident((d, f)), resident((f, d)), vec, gain],
        out_specs=row,
        out_shape=jax.ShapeDtypeStruct((b, s, d), F32),
        compiler_params=_cparams("parallel", "parallel"),
        name="layer_tail",
    )(*o_list, *lse_list, wo, x, gt1, g1.reshape(1, d), g2.reshape(1, d), sc2, sh2, w1, w2, gt2,
      g3.reshape(1, d))


def _banded_kernel(slopes_ref, sink_ref, q_ref, k_ref, v_ref, tab_ref, *out_refs,
                   half, tq_sub, n_sub, n_u, dil, seq, use_sink, emit_lse):
    o_ref = out_refs[0]
    n_heads = 2 * n_u
    head0 = pl.program_id(1) * n_heads
    span = tq_sub + 2 * half
    lane = lax.broadcasted_iota(jnp.int32, (tq_sub, LANES), 1)
    base = (lax.broadcasted_iota(jnp.int32, (tq_sub, span), 1)
            - lax.broadcasted_iota(jnp.int32, (tq_sub, span), 0) - half)
    def block(sb):
        if dil == 1:
            r, q0 = 0, (pl.program_id(2) * n_sub + sb) * tq_sub
            qrows = slice(sb * tq_sub, (sb + 1) * tq_sub)
            rows = qrows
        else:
            r, q0 = sb, pl.program_id(2) * tq_sub
            qrows = slice(0, tq_sub)
            rows = pl.ds(r, tq_sub, stride=dil)
        start = pl.multiple_of(jnp.clip(q0 - half, 0, seq - span), HEAD_DIM)
        return r, qrows, rows, q0, start

    def scores(sb):
        r, qrows, _, _, start = block(sb)
        qs = []
        for u in range(n_u):
            qu = q_ref[0, r, qrows, u * LANES:(u + 1) * LANES]
            qs += [_head_select(qu, lane, 0), _head_select(qu, lane, 1)]
        return lax.dot_general(jnp.concatenate(qs, axis=0), k_ref[0, r, pl.ds(start, span), :], _NT,
                               preferred_element_type=F32)

    def finish(sb, s_all, unclamped):
        r, _, rows, q0, start = block(sb)
        vw = v_ref[0, r, pl.ds(start, span), :]
        if not unclamped:
            dist = jnp.abs(base + (start - q0 + half))
            valid = dist <= half
            distf = dist.astype(F32)
        prs, dens, lses = [], [], []
        for hd in range(n_heads):
            s = s_all[hd * tq_sub:(hd + 1) * tq_sub]
            if unclamped:
                s = s + tab_ref[hd]
            else:
                s = jnp.where(valid, s - slopes_ref[head0 + hd] * distf, NEG)
            m = jnp.max(s, axis=-1, keepdims=True)
            if use_sink:
                sink = sink_ref[head0 + hd]
                m = jnp.maximum(m, sink)
            pr = jnp.exp2(s - m)
            den = jnp.sum(pr, axis=-1, keepdims=True)
            if use_sink:
                den = den + jnp.exp2(sink - m)
            prs.append(pr.astype(BF16))
            dens.append(den)
            lses.append(m + jnp.log2(den))
        acc_all = jnp.dot(jnp.concatenate(prs, axis=0), vw, preferred_element_type=F32)
        for u in range(n_u):
            r0 = acc_all[(2 * u) * tq_sub:(2 * u + 1) * tq_sub] / dens[2 * u]
            r1 = acc_all[(2 * u + 1) * tq_sub:(2 * u + 2) * tq_sub] / dens[2 * u + 1]
            cols = slice(u * LANES, (u + 1) * LANES)
            o_ref[0, rows, cols] = jnp.where(lane < HEAD_DIM, r0, r1).astype(o_ref.dtype)
            if emit_lse:
                out_refs[1][0, rows, cols] = jnp.where(lane < HEAD_DIM, lses[2 * u], lses[2 * u + 1])

    def run(unclamped):
        ahead = min(2, n_sub)
        pending = [scores(sb) for sb in range(ahead)]
        for sb in range(n_sub):
            if sb + ahead < n_sub:
                pending.append(scores(sb + ahead))
            finish(sb, pending.pop(0), unclamped)

    first_q0 = block(0)[3]
    last_q0 = block(n_sub - 1)[3]
    unclamped = (first_q0 >= half) & (last_q0 + tq_sub + half <= seq)
    pl.when(unclamped)(functools.partial(run, True))
    pl.when(jnp.logical_not(unclamped))(functools.partial(run, False))


def _banded_bias_table(slopes, half, tq_sub):
    span = tq_sub + 2 * half
    dist = jnp.abs(jnp.arange(span)[None, :] - jnp.arange(tq_sub)[:, None] - half)
    return jnp.where(dist <= half, -slopes[:, None, None] * dist.astype(F32), NEG)


def _banded_call(qkv, q_blk, k_blk, v_blk, half, dil, n_groups, n_u, slopes, sinks, use_sink, emit_lse):
    b, _, sd, _ = qkv.shape
    tq_sub = 128
    n_sub = min(8, sd // tq_sub) if dil == 1 else dil
    tq_stream = tq_sub * n_sub // dil
    wq = n_u * LANES
    assert dil == 1 or (n_u == 1 and emit_lse)
    out_dtype = F32 if emit_lse else BF16
    out_spec = pl.BlockSpec((1, tq_sub * n_sub, wq), lambda bi, p, i: (bi, i, p))
    out_shape = [jax.ShapeDtypeStruct((b, sd * dil, n_groups * wq), out_dtype)]
    out_specs = [out_spec]
    if emit_lse:
        out_shape.append(out_shape[0])
        out_specs.append(out_spec)
    return pl.pallas_call(
        functools.partial(_banded_kernel, half=half, tq_sub=tq_sub, n_sub=n_sub, n_u=n_u, dil=dil,
                          seq=sd, use_sink=use_sink, emit_lse=emit_lse),
        grid=(b, n_groups, sd // tq_stream),
        in_specs=[
            _smem_spec(), _smem_spec(),
            pl.BlockSpec((1, dil, tq_stream, wq), lambda bi, p, i: (bi, 0, i, q_blk(p))),
            pl.BlockSpec((1, dil, sd, LANES), lambda bi, p, i: (bi, 0, 0, k_blk(p))),
            pl.BlockSpec((1, dil, sd, LANES), lambda bi, p, i: (bi, 0, 0, v_blk(p))),
            pl.BlockSpec((2 * n_u, tq_sub, tq_sub + 2 * half), lambda bi, p, i: (p, 0, 0)),
        ],
        out_specs=out_specs,
        out_shape=out_shape,
        compiler_params=_cparams("parallel", "parallel", "arbitrary"),
        name="banded_attn",
    )(slopes, sinks, qkv, qkv, qkv, _banded_bias_table(slopes, half, tq_sub))


def _nbr_kernel(q_ref, k_ref, v_ref, t_ref, o_ref, *, rows_per_tile, n_rows):
    i = pl.program_id(2)
    nkeys = WIN_R * GRID_W
    lane = lax.broadcasted_iota(jnp.int32, (GRID_W, LANES), 1)
    def band(rr):
        r = i * rows_per_tile + rr
        rs = jnp.clip(r - WIN_R // 2, 0, n_rows - WIN_R)
        off = rs - r + (WIN_R - 1)
        return pl.multiple_of(rs * GRID_W, GRID_W), off

    def scores(rr):
        start, _ = band(rr)
        q = q_ref[0, rr * GRID_W:(rr + 1) * GRID_W, :]
        q2 = jnp.concatenate([_head_select(q, lane, 0), _head_select(q, lane, 1)], axis=0)
        return lax.dot_general(q2, k_ref[0, pl.ds(start, nkeys), :], _NT, preferred_element_type=F32)

    def finish(rr, s_all):
        start, off = band(rr)
        prs, dens = [], []
        for hh in range(2):
            s = s_all[hh * GRID_W:(hh + 1) * GRID_W] + t_ref[0, hh, off]
            m = jnp.max(s, axis=-1, keepdims=True)
            pr = jnp.exp2(s - m)
            dens.append(jnp.sum(pr, axis=-1, keepdims=True))
            prs.append(pr.astype(BF16))
        acc = jnp.dot(jnp.concatenate(prs, axis=0), v_ref[0, pl.ds(start, nkeys), :],
                      preferred_element_type=F32)
        o_ref[0, rr * GRID_W:(rr + 1) * GRID_W, :] = jnp.where(
            lane < HEAD_DIM, acc[:GRID_W] / dens[0], acc[GRID_W:] / dens[1]).astype(o_ref.dtype)

    ahead = min(3, rows_per_tile)
    pending = [scores(rr) for rr in range(ahead)]
    for rr in range(rows_per_tile):
        if rr + ahead < rows_per_tile:
            pending.append(scores(rr + ahead))
        finish(rr, pending.pop(0))


def _nbr_bias_table(rel_bias):
    col = jnp.arange(GRID_W)
    cs = jnp.clip(col - WIN_C // 2, 0, GRID_W - WIN_C)
    col_valid = (col[None, :] >= cs[:, None]) & (col[None, :] < cs[:, None] + WIN_C)
    dc_idx = jnp.clip(col[None, :] - col[:, None] + WIN_C - 1, 0, 2 * WIN_C - 2)
    dr = jnp.arange(WIN_R)[:, None] + jnp.arange(WIN_R)[None, :]
    tab = rel_bias.astype(F32)[:, dr] * LOG2E
    tab = tab[..., dc_idx]
    tab = jnp.where(col_valid, tab, NEG)
    tab = jnp.transpose(tab, (0, 1, 3, 2, 4))
    h = tab.shape[0]
    return tab.reshape(h // 2, 2, WIN_R, GRID_W, WIN_R * GRID_W)


def _nbr_call(qkv, table):
    b, s, _ = qkv.shape
    n_tiles = B_HEADS // 2
    n_rows = s // GRID_W
    rows_per_tile = min(32, n_rows)
    tq = rows_per_tile * GRID_W
    kv = lambda off: pl.BlockSpec((1, s, LANES), lambda bi, p, i: (bi, 0, off + p))
    return pl.pallas_call(
        functools.partial(_nbr_kernel, rows_per_tile=rows_per_tile, n_rows=n_rows),
        grid=(b, n_tiles, s // tq),
        in_specs=[
            pl.BlockSpec((1, tq, LANES), lambda bi, p, i: (bi, i, p)),
            kv(n_tiles), kv(2 * n_tiles),
            pl.BlockSpec((1, 2, WIN_R, GRID_W, WIN_R * GRID_W), lambda bi, p, i: (p, 0, 0, 0, 0)),
        ],
        out_specs=pl.BlockSpec((1, tq, LANES), lambda bi, p, i: (bi, i, p)),
        out_shape=jax.ShapeDtypeStruct((b, s, n_tiles * LANES), BF16),
        compiler_params=_cparams("parallel", "parallel", "arbitrary"),
        name="nbr_attn",
    )(qkv, qkv, qkv, table)


_POS_SPLIT = 32


def _diff_kernel(slopes_ref, lam_ref, q_ref, k_ref, vt_ref, kp_ref, g_ref, o_ref,
                 qa_s, m_s, acc_s, s_s, p_s, al_s, rb_s, mx_s, *, t, n_tiles, post_scale):
    h = pl.program_id(1)
    i = pl.program_id(2)
    slope = slopes_ref[h]
    ones_rows = jnp.ones((BF16_ROWS, t), BF16)

    q = q_ref[0]
    lane = lax.broadcasted_iota(jnp.int32, (t, LANES), 1)
    c = jnp.full((t, LANES), -slope, F32)
    c_hi = c.astype(BF16).astype(F32)
    aug = jnp.where(lane < 2, c_hi, jnp.where(lane < 4, c - c_hi, 0.0))
    for mp in range(2):
        qa = jnp.concatenate([_head_select(q, lane, mp).astype(F32), aug], axis=1)
        qa_s[mp] = qa.T.astype(BF16)

    def tile_of(pos):
        if pos == 0:
            return i, None
        after = (pos - 1) >= i
        return pos - 1 + jnp.where(after, 1, 0), after

    m_s[...] = jnp.full(m_s.shape, NEG, F32)
    acc_s[...] = jnp.zeros(acc_s.shape, F32)

    def stage_scores(pos, slot):
        j, after = tile_of(pos)
        kt = k_ref[0, pl.ds(pl.multiple_of(j * t, t), t), :]
        if pos == 0:
            kpos = lax.broadcasted_iota(jnp.int32, (t, t), 0)
            qpos = lax.broadcasted_iota(jnp.int32, (t, t), 1)
            bias = slope * jnp.abs(kpos - qpos).astype(F32)
            rb_s[slot] = jnp.zeros((1, t), F32)
        else:
            ka = jnp.concatenate([kt, kp_ref[jnp.where(after, 0, 1)]], axis=1)
            sgn = jnp.where(after, 1.0, -1.0).astype(F32)
            qrow = lax.broadcasted_iota(jnp.int32, (1, t), 1).astype(F32)
            blk = (jnp.abs(j - i) * t).astype(F32)
            rb_s[slot] = slope * (sgn * qrow - blk)
        for mp in range(2):
            if pos == 0:
                s = jnp.dot(kt, qa_s[mp, :LANES, :], preferred_element_type=F32) - bias
            else:
                s = jnp.dot(ka, qa_s[mp], preferred_element_type=F32)
            s_s[slot, mp] = s
            mx_s[slot, mp] = jnp.max(s, axis=0, keepdims=True)

    def stage_softmax(slot):
        rb = rb_s[slot]
        for mp in range(2):
            m_old = m_s[mp]
            m_new = jnp.maximum(m_old, mx_s[slot, mp] + rb)
            al_s[slot, mp] = jnp.exp2(m_old - m_new)
            p_s[slot, mp] = jnp.exp2(s_s[slot, mp] - (m_new - rb)).astype(BF16)
            m_s[mp] = m_new

    def stage_values(pos, slot):
        j, _ = tile_of(pos)
        va = jnp.concatenate([vt_ref[0, 0, j], ones_rows], axis=0)
        for mp in range(2):
            acc_s[mp] = al_s[slot, mp] * acc_s[mp] + jnp.dot(va, p_s[slot, mp], preferred_element_type=F32)

    for n in range(n_tiles + 2):
        if 2 <= n:
            stage_values(n - 2, n % 2)
        if 1 <= n <= n_tiles:
            stage_softmax(1 - n % 2)
        if n < n_tiles:
            stage_scores(n, n % 2)

    o0 = acc_s[0, :LANES] / acc_s[0, LANES:LANES + 1]
    o1 = acc_s[1, :LANES] / acc_s[1, LANES:LANES + 1]
    o = o0 - lam_ref[0] * o1
    ms = jnp.mean(o * o, axis=0, keepdims=True)
    y = (o * lax.rsqrt(ms + EPS) * g_ref[...]) * post_scale
    o_ref[0] = y.T.astype(o_ref.dtype)


def _key_offset_tiles(t):
    r = jnp.arange(t)
    hi = (r // _POS_SPLIT * _POS_SPLIT).astype(F32)
    lo = (r % _POS_SPLIT).astype(F32)
    tile = jnp.zeros((t, LANES), F32)
    tile = tile.at[:, 0].set(hi).at[:, 1].set(lo).at[:, 2].set(hi).at[:, 3].set(lo)
    return jnp.stack([tile, -tile]).astype(BF16)


def _diff_tile(s):
    return min(512, s)


def _diff_call(qk, vt, slopes, lam, subln, post_scale):
    b, s, _ = qk.shape
    t = _diff_tile(s)
    n_tiles = s // t
    assert t <= _POS_SPLIT * BF16_ROWS
    return pl.pallas_call(
        functools.partial(_diff_kernel, t=t, n_tiles=n_tiles, post_scale=post_scale),
        grid=(b, C_HEADS, n_tiles),
        in_specs=[
            _smem_spec(), _smem_spec(),
            pl.BlockSpec((1, t, LANES), lambda bi, h, i: (bi, i, h)),
            pl.BlockSpec((1, s, LANES), lambda bi, h, i: (bi, 0, C_HEADS + h)),
            pl.BlockSpec((1, 1, n_tiles, LANES, t), lambda bi, h, i: (bi, h, 0, 0, 0)),
            pl.BlockSpec((2, t, LANES), lambda bi, h, i: (0, 0, 0)),
            pl.BlockSpec((LANES, 1), lambda bi, h, i: (0, 0)),
        ],
        out_specs=pl.BlockSpec((1, t, LANES), lambda bi, h, i: (bi, i, h)),
        out_shape=jax.ShapeDtypeStruct((b, s, C_HEADS * LANES), BF16),
        scratch_shapes=[
            pltpu.VMEM((2, 2 * LANES, t), BF16),
            pltpu.VMEM((2, 1, t), F32),
            pltpu.VMEM((2, LANES + BF16_ROWS, t), F32),
            pltpu.VMEM((2, 2, t, t), F32),
            pltpu.VMEM((2, 2, t, t), BF16),
            pltpu.VMEM((2, 2, 1, t), F32),
            pltpu.VMEM((2, 1, t), F32),
            pltpu.VMEM((2, 2, 1, t), F32),
        ],
        compiler_params=_cparams("parallel", "parallel", "parallel"),
        name="diff_attn",
    )(slopes, lam, qk, qk, vt, _key_offset_tiles(t), subln.astype(F32).reshape(LANES, 1))


def _mixer_a(x, g, sc, sh, wqkv, sink):
    d = x.shape[-1]
    nq = A_HEADS * HEAD_DIM
    nkv = A_KV_HEADS * HEAD_DIM

    def dup(wpart):
        wpart = wpart.reshape(d, A_KV_HEADS, 1, HEAD_DIM)
        return jnp.broadcast_to(wpart, (d, A_KV_HEADS, 2, HEAD_DIM)).reshape(d, 2 * nkv).astype(BF16)

    w = jnp.concatenate([_scaled_q_weights(wqkv[:, :nq]), dup(wqkv[:, nq:nq + nkv]),
                         dup(wqkv[:, nq + nkv:])], axis=1)
    qkv = _qkv_call(x, g, sc, sh, w)
    q_tiles = nq // LANES
    kv_tiles = 2 * nkv // LANES
    pairs_per_kv = A_HEADS // A_KV_HEADS // 2
    return _banded_call(
        qkv, q_blk=lambda kv: kv, k_blk=lambda kv: q_tiles + kv, v_blk=lambda kv: q_tiles + kv_tiles + kv,
        half=A_WINDOW, dil=1, n_groups=A_KV_HEADS, n_u=pairs_per_kv, slopes=_alibi_slopes(A_HEADS) * LOG2E,
        sinks=sink.astype(F32) * LOG2E, use_sink=True, emit_lse=False), []


def _mixer_b(x, g, sc, sh, wqkv, rel_bias):
    nq = B_HEADS * HEAD_DIM
    w = jnp.concatenate([_scaled_q_weights(wqkv[:, :nq]), wqkv[:, nq:].astype(BF16)], axis=1)
    qkv = _qkv_call(x, g, sc, sh, w)[:, 0]
    return [_nbr_call(qkv, _nbr_bias_table(rel_bias))], []


def _mixer_c(x, g, sc, sh, wqkv, lam, subln, lam_init):
    nq = 2 * C_HEADS * HEAD_DIM
    w = jnp.concatenate([_scaled_q_weights(wqkv[:, :nq]), wqkv[:, nq:2 * nq].astype(BF16)], axis=1)
    qk, vt = _qkv_call(x, g, sc, sh, w, wqkv[:, 2 * nq:].T.astype(BF16), tile=_diff_tile(x.shape[1]))
    qk = qk[:, 0]
    lf = lam.astype(F32)
    lam_full = jnp.exp(jnp.sum(lf[0] * lf[1])) - jnp.exp(jnp.sum(lf[2] * lf[3])) + lam_init
    return [_diff_call(qk, vt, _alibi_slopes(C_HEADS) * LOG2E, lam_full.reshape(1), subln,
                       1.0 - lam_init)], []


def _mixer_d(x, g, sc, sh, wqkv):
    d = x.shape[-1]
    nh = D_HEADS * HEAD_DIM
    n_tiles = D_HEADS // 2
    outs, lses = [], []
    for gi, (win, dil) in enumerate(D_GROUPS):
        wg = wqkv[:, gi * 3 * nh:(gi + 1) * 3 * nh]
        w = jnp.concatenate([_scaled_q_weights(wg[:, :nh]), wg[:, nh:].astype(BF16)], axis=1)
        qkv = _qkv_call(x, g, sc, sh, w, dil=dil)
        o, lse = _banded_call(
            qkv, q_blk=lambda p: p, k_blk=lambda p: n_tiles + p, v_blk=lambda p: 2 * n_tiles + p,
            half=win // (2 * dil), dil=dil, n_groups=n_tiles, n_u=1,
            slopes=_alibi_slopes(D_HEADS) * (LOG2E * dil), sinks=jnp.zeros((D_HEADS,), F32),
            use_sink=False, emit_lse=True)
        outs.append(o)
        lses.append(lse)
    return outs, lses


def _trunk(x, mod, g_mix_pre, g_mix_post, g_mlp_pre, g_mlp_post, mlp_w1, mlp_w2,
           a_wqkv, a_wo, a_sink, b_wqkv, b_wo, b_rel_bias, c_wqkv, c_wo, c_lambda, c_subln, d_wqkv, d_wo):
    depth = mod.shape[0]
    d = x.shape[-1]
    for i in range(depth):
        sh1, sc1, gt1, sh2, sc2, gt2 = [mod[i, :, None, k * d:(k + 1) * d] for k in range(6)]
        kind, occ = i % N_MIXERS, i // N_MIXERS
        if kind == 0:
            (o, lse), wo = _mixer_a(x, g_mix_pre[i], sc1, sh1, a_wqkv[occ], a_sink[occ]), a_wo[occ]
        elif kind == 1:
            (o, lse), wo = _mixer_b(x, g_mix_pre[i], sc1, sh1, b_wqkv[occ], b_rel_bias[occ]), b_wo[occ]
        elif kind == 2:
            (o, lse), wo = _mixer_c(x, g_mix_pre[i], sc1, sh1, c_wqkv[occ], c_lambda[occ], c_subln[occ],
                                    0.8 - 0.6 * math.exp(-0.3 * i)), c_wo[occ]
        else:
            (o, lse), wo = _mixer_d(x, g_mix_pre[i], sc1, sh1, d_wqkv[occ]), d_wo[occ]
        x = _tail_call(o, lse, wo.astype(BF16), x, gt1, g_mix_post[i], g_mlp_pre[i], sc2, sh2,
                       mlp_w1[i].astype(BF16), mlp_w2[i].astype(BF16), gt2, g_mlp_post[i])
    return x


def kernel(x_prompt, x_sample, c_prompt, c_sample, ada_w, ada_b, g_mix_pre, g_mix_post, g_mlp_pre, g_mlp_post, mlp_w1, mlp_w2, a_wqkv, a_wo, a_sink, b_wqkv, b_wo, b_rel_bias, c_wqkv, c_wo, c_lambda, c_subln, d_wqkv, d_wo):
    bp, bs = c_prompt.shape[0], c_sample.shape[0]
    c_all = jnp.concatenate([c_prompt, c_sample], axis=0)
    pad = (-c_all.shape[0]) % 8
    c_all = jnp.pad(c_all, ((0, pad), (0, 0)))
    mod = _adaln_call(c_all, ada_w, ada_b)
    params = (g_mix_pre, g_mix_post, g_mlp_pre, g_mlp_post, mlp_w1, mlp_w2, a_wqkv, a_wo, a_sink,
              b_wqkv, b_wo, b_rel_bias, c_wqkv, c_wo, c_lambda, c_subln, d_wqkv, d_wo)
    y_prompt = _trunk(x_prompt, mod[:, :bp], *params)
    y_sample = _trunk(x_sample, mod[:, bp:bp + bs], *params)
    return (y_prompt, y_sample)
```

```python
import functools
import math

import jax
import jax.numpy as jnp
from jax import lax
from jax.experimental import pallas as pl
from jax.experimental.pallas import tpu as pltpu

F32 = jnp.float32
BF16 = jnp.bfloat16

D_MODEL = 1024
HEAD_DIM = 64
LANES = 128
BF16_ROWS = 16
D_FF = 4 * D_MODEL
EPS = 1e-6
NEG = -1e30
LOG2E = math.log2(math.e)
Q_SCALE = HEAD_DIM ** -0.5 * LOG2E
N_MIXERS = 4

A_HEADS, A_KV_HEADS, A_WINDOW = 16, 4, 128
B_HEADS, GRID_W, WIN_R, WIN_C = 16, 64, 8, 16
C_HEADS = 8
D_GROUPS = ((128, 1), (512, 4), (2048, 16))
D_HEADS = 8

VMEM_LIMIT_BYTES = 56 * 1024 * 1024

_NT = (((1,), (1,)), ((), ()))


def _cparams(*sem):
    return pltpu.CompilerParams(dimension_semantics=sem, vmem_limit_bytes=VMEM_LIMIT_BYTES)


def _smem_spec():
    return pl.BlockSpec(memory_space=pltpu.SMEM)


def _alibi_slopes(n):
    return 2.0 ** (-8.0 * jnp.arange(1, n + 1, dtype=F32) / n)


def _modulated_norm(x, g, sc, sh):
    ms = jnp.mean(x * x, axis=-1, keepdims=True)
    return (x * lax.rsqrt(ms + EPS) * g) * (1.0 + sc) + sh


def _gated_norm_residual(x, h, g, gt):
    ms = jnp.mean(h * h, axis=-1, keepdims=True)
    return x + gt * (h * lax.rsqrt(ms + EPS) * g)


def _head_select(q, lane, hh):
    keep = (lane >= HEAD_DIM) if hh else (lane < HEAD_DIM)
    return jnp.where(keep, q, jnp.zeros_like(q))


def _scaled_q_weights(wq):
    return (wq * Q_SCALE).astype(BF16)


def _adaln_kernel(c_ref, w_ref, b_ref, o_ref):
    c = c_ref[...]
    ca = (c * jax.nn.sigmoid(c)).astype(BF16)
    o_ref[0] = jnp.dot(ca, w_ref[0].astype(BF16), preferred_element_type=F32) + b_ref[0]


def _adaln_call(c_pad, ada_w, ada_b):
    depth, d, n = ada_w.shape
    bp = c_pad.shape[0]
    tn = 1536
    return pl.pallas_call(
        _adaln_kernel,
        grid=(depth, n // tn),
        in_specs=[
            pl.BlockSpec((bp, d), lambda l, j: (0, 0)),
            pl.BlockSpec((1, d, tn), lambda l, j: (l, 0, j)),
            pl.BlockSpec((1, 1, tn), lambda l, j: (l, 0, j)),
        ],
        out_specs=pl.BlockSpec((1, bp, tn), lambda l, j: (l, 0, j)),
        out_shape=jax.ShapeDtypeStruct((depth, bp, n), F32),
        compiler_params=_cparams("parallel", "parallel"),
        name="adaln",
    )(c_pad, ada_w, ada_b.reshape(depth, 1, n))


def _qkv_kernel(*refs, n_chunk, has_t, dils):
    nw = len(dils)
    x_ref, g_ref, sc_ref, sh_ref = refs[:4]
    w_refs = refs[4:4 + nw]
    pos = 4 + nw + (1 if has_t else 0)
    o_refs = refs[pos:pos + nw]
    scratch = refs[pos + nw + (1 if has_t else 0):]
    hn = _modulated_norm(x_ref[0], g_ref[...], sc_ref[0], sh_ref[0])
    h_nat = hn.astype(BF16)
    if scratch:
        slab_s, h_s = scratch
        for c in range(slab_s.shape[0]):
            slab_s[c] = hn[:, c * LANES:(c + 1) * LANES]
    for w_ref, o_ref, dil in zip(w_refs, o_refs, dils):
        tm = o_ref.shape[2]
        if dil == 1:
            h = h_nat
        else:
            for rho in range(dil):
                for c in range(slab_s.shape[0]):
                    h_s[rho * tm:(rho + 1) * tm, c * LANES:(c + 1) * LANES] = (
                        slab_s[c, pl.ds(rho, tm, stride=dil), :].astype(BF16))
            h = h_s[...]
        for c0 in range(0, w_ref.shape[1], n_chunk):
            oc = jnp.dot(h, w_ref[:, c0:c0 + n_chunk], preferred_element_type=F32).astype(o_ref.dtype)
            o_ref[0, :, :, c0:c0 + n_chunk] = oc.reshape(dil, tm, n_chunk)
    if has_t:
        wt_ref, ot_ref = refs[4 + nw], refs[pos + nw]
        ht = lax.dot_general(wt_ref[...], h_nat, _NT, preferred_element_type=F32).astype(ot_ref.dtype)
        ot_ref[0, :, 0] = ht.reshape(ot_ref.shape[1], LANES, ht.shape[1])


def _qkv_call(x, g, sc, sh, ws, dils=(1,), wt=None, tile=512):
    b, s, d = x.shape
    tile = min(tile, s)
    const = lambda shape: pl.BlockSpec(shape, lambda bi, i: (0, 0))
    in_specs = [
        pl.BlockSpec((1, tile, d), lambda bi, i: (bi, i, 0)),
        const((1, d)),
        pl.BlockSpec((1, 1, d), lambda bi, i: (bi, 0, 0)),
        pl.BlockSpec((1, 1, d), lambda bi, i: (bi, 0, 0)),
    ] + [const(w.shape) for w in ws]
    out_shape, out_specs = [], []
    for w, dil in zip(ws, dils):
        tm = tile // dil
        assert tm % BF16_ROWS == 0
        out_shape.append(jax.ShapeDtypeStruct((b, dil, s // dil, w.shape[1]), BF16))
        out_specs.append(pl.BlockSpec((1, dil, tm, w.shape[1]), lambda bi, i: (bi, 0, i, 0)))
    args = [x, g.reshape(1, d), sc, sh, *ws]
    if wt is not None:
        nt = wt.shape[0]
        in_specs.append(const((nt, d)))
        out_shape.append(jax.ShapeDtypeStruct((b, nt // LANES, s // tile, LANES, tile), BF16))
        out_specs.append(pl.BlockSpec((1, nt // LANES, 1, LANES, tile), lambda bi, i: (bi, 0, i, 0, 0)))
        args.append(wt)
    scratch = []
    if max(dils) > 1:
        scratch = [pltpu.VMEM((d // LANES, tile, LANES), F32), pltpu.VMEM((tile, d), BF16)]
    return pl.pallas_call(
        functools.partial(_qkv_kernel, n_chunk=512, has_t=wt is not None, dils=tuple(dils)),
        grid=(b, s // tile),
        in_specs=in_specs,
        out_specs=out_specs,
        out_shape=out_shape,
        scratch_shapes=scratch,
        compiler_params=_cparams("parallel", "parallel"),
        name="qkv_proj",
    )(*args)


def _merge_groups(o_refs, lse_refs):
    m = lse_refs[0][0]
    for r in lse_refs[1:]:
        m = jnp.maximum(m, r[0])
    num = jnp.zeros(m.shape, F32)
    den = jnp.zeros(m.shape, F32)
    for o_r, l_r in zip(o_refs, lse_refs):
        e = jnp.exp2(l_r[0] - m)
        den = den + e
        num = num + e * o_r[0]
    return num / den


def _tail_kernel(*refs, n_mix, f_chunk):
    n_o = max(n_mix, 1)
    o_refs, lse_refs = refs[:n_o], refs[n_o:n_o + n_mix]
    (wo_ref, x_ref, gt1_ref, g1_ref, g2_ref, sc_ref, sh_ref, w1_ref, w2_ref, gt2_ref, g3_ref,
     y_ref) = refs[n_o + n_mix:]
    o = _merge_groups(o_refs, lse_refs).astype(BF16) if n_mix else o_refs[0][0]
    x1 = _gated_norm_residual(x_ref[0], jnp.dot(o, wo_ref[...], preferred_element_type=F32),
                              g1_ref[...], gt1_ref[0])
    h = _modulated_norm(x1, g2_ref[...], sc_ref[0], sh_ref[0]).astype(BF16)
    acc = jnp.zeros(x1.shape, F32)
    for c0 in range(0, w1_ref.shape[1], f_chunk):
        a = jnp.dot(h, w1_ref[:, c0:c0 + f_chunk], preferred_element_type=F32)
        a = jnp.maximum(a, 0.0)
        a = (a * a).astype(BF16)
        acc = acc + jnp.dot(a, w2_ref[c0:c0 + f_chunk, :], preferred_element_type=F32)
    y_ref[0] = _gated_norm_residual(x1, acc, g3_ref[...], gt2_ref[0])


def _tail_call(o_list, lse_list, wo, x, gt1, g1, g2, sc2, sh2, w1, w2, gt2, g3):
    b, s, d = x.shape
    kd = wo.shape[0]
    f = w1.shape[1]
    n_mix = len(lse_list)
    tm = min(256 if n_mix else 512, s)
    tile = pl.BlockSpec((1, tm, kd), lambda bi, i: (bi, i, 0))
    row = pl.BlockSpec((1, tm, d), lambda bi, i: (bi, i, 0))
    vec = pl.BlockSpec((1, 1, d), lambda bi, i: (bi, 0, 0))
    gain = pl.BlockSpec((1, d), lambda bi, i: (0, 0))

    def resident(shape):
        return pl.BlockSpec(shape, lambda bi, i: (0, 0), pipeline_mode=pl.Buffered(1))

    return pl.pallas_call(
        functools.partial(_tail_kernel, n_mix=n_mix, f_chunk=1024),
        grid=(b, s // tm),
        in_specs=[tile] * (len(o_list) + n_mix) + [
            resident((kd, d)), row, vec, gain, gain, vec, vec, resident((d, f)), resident((f, d)), vec, gain],
        out_specs=row,
        out_shape=jax.ShapeDtypeStruct((b, s, d), F32),
        compiler_params=_cparams("parallel", "parallel"),
        name="layer_tail",
    )(*o_list, *lse_list, wo, x, gt1, g1.reshape(1, d), g2.reshape(1, d), sc2, sh2, w1, w2, gt2,
      g3.reshape(1, d))


def _banded_kernel(slopes_ref, sink_ref, q_ref, k_ref, v_ref, tab_ref, *out_refs,
                   half, tq_sub, n_sub, n_u, dil, seq, use_sink, emit_lse):
    o_ref = out_refs[0]
    n_heads = 2 * n_u
    head0 = pl.program_id(1) * n_heads
    span = tq_sub + 2 * half
    lane = lax.broadcasted_iota(jnp.int32, (tq_sub, LANES), 1)
    base = (lax.broadcasted_iota(jnp.int32, (tq_sub, span), 1)
            - lax.broadcasted_iota(jnp.int32, (tq_sub, span), 0) - half)
    def block(sb):
        if dil == 1:
            r, q0 = 0, (pl.program_id(2) * n_sub + sb) * tq_sub
            qrows = slice(sb * tq_sub, (sb + 1) * tq_sub)
            rows = qrows
        else:
            r, q0 = sb, pl.program_id(2) * tq_sub
            qrows = slice(0, tq_sub)
            rows = pl.ds(r, tq_sub, stride=dil)
        start = pl.multiple_of(jnp.clip(q0 - half, 0, seq - span), HEAD_DIM)
        return r, qrows, rows, q0, start

    def scores(sb):
        r, qrows, _, _, start = block(sb)
        qs = []
        for u in range(n_u):
            qu = q_ref[0, r, qrows, u * LANES:(u + 1) * LANES]
            qs += [_head_select(qu, lane, 0), _head_select(qu, lane, 1)]
        return lax.dot_general(jnp.concatenate(qs, axis=0), k_ref[0, r, pl.ds(start, span), :], _NT,
                               preferred_element_type=F32)

    def finish(sb, s_all, unclamped):
        r, _, rows, q0, start = block(sb)
        vw = v_ref[0, r, pl.ds(start, span), :]
        if not unclamped:
            dist = jnp.abs(base + (start - q0 + half))
            valid = dist <= half
            distf = dist.astype(F32)
        prs, dens, lses = [], [], []
        for hd in range(n_heads):
            s = s_all[hd * tq_sub:(hd + 1) * tq_sub]
            if unclamped:
                s = s + tab_ref[hd]
            else:
                s = jnp.where(valid, s - slopes_ref[head0 + hd] * distf, NEG)
            m = jnp.max(s, axis=-1, keepdims=True)
            if use_sink:
                sink = sink_ref[head0 + hd]
                m = jnp.maximum(m, sink)
            pr = jnp.exp2(s - m)
            den = jnp.sum(pr, axis=-1, keepdims=True)
            if use_sink:
                den = den + jnp.exp2(sink - m)
            prs.append(pr.astype(BF16))
            dens.append(den)
            lses.append(m + jnp.log2(den))
        acc_all = jnp.dot(jnp.concatenate(prs, axis=0), vw, preferred_element_type=F32)
        for u in range(n_u):
            r0 = acc_all[(2 * u) * tq_sub:(2 * u + 1) * tq_sub] / dens[2 * u]
            r1 = acc_all[(2 * u + 1) * tq_sub:(2 * u + 2) * tq_sub] / dens[2 * u + 1]
            cols = slice(u * LANES, (u + 1) * LANES)
            o_ref[0, rows, cols] = jnp.where(lane < HEAD_DIM, r0, r1).astype(o_ref.dtype)
            if emit_lse:
                out_refs[1][0, rows, cols] = jnp.where(lane < HEAD_DIM, lses[2 * u], lses[2 * u + 1])

    def run(unclamped):
        ahead = min(2, n_sub)
        pending = [scores(sb) for sb in range(ahead)]
        for sb in range(n_sub):
            if sb + ahead < n_sub:
                pending.append(scores(sb + ahead))
            finish(sb, pending.pop(0), unclamped)

    first_q0 = block(0)[3]
    last_q0 = block(n_sub - 1)[3]
    unclamped = (first_q0 >= half) & (last_q0 + tq_sub + half <= seq)
    pl.when(unclamped)(functools.partial(run, True))
    pl.when(jnp.logical_not(unclamped))(functools.partial(run, False))


def _banded_bias_table(slopes, half, tq_sub):
    span = tq_sub + 2 * half
    dist = jnp.abs(jnp.arange(span)[None, :] - jnp.arange(tq_sub)[:, None] - half)
    return jnp.where(dist <= half, -slopes[:, None, None] * dist.astype(F32), NEG)


def _banded_call(qkv, q_blk, k_blk, v_blk, half, dil, n_groups, n_u, slopes, sinks, use_sink, emit_lse):
    b, _, sd, _ = qkv.shape
    tq_sub = 128
    n_sub = min(8, sd // tq_sub) if dil == 1 else dil
    tq_stream = tq_sub * n_sub // dil
    wq = n_u * LANES
    assert dil == 1 or (n_u == 1 and emit_lse)
    out_dtype = F32 if emit_lse else BF16
    out_spec = pl.BlockSpec((1, tq_sub * n_sub, wq), lambda bi, p, i: (bi, i, p))
    out_shape = [jax.ShapeDtypeStruct((b, sd * dil, n_groups * wq), out_dtype)]
    out_specs = [out_spec]
    if emit_lse:
        out_shape.append(out_shape[0])
        out_specs.append(out_spec)
    return pl.pallas_call(
        functools.partial(_banded_kernel, half=half, tq_sub=tq_sub, n_sub=n_sub, n_u=n_u, dil=dil,
                          seq=sd, use_sink=use_sink, emit_lse=emit_lse),
        grid=(b, n_groups, sd // tq_stream),
        in_specs=[
            _smem_spec(), _smem_spec(),
            pl.BlockSpec((1, dil, tq_stream, wq), lambda bi, p, i: (bi, 0, i, q_blk(p))),
            pl.BlockSpec((1, dil, sd, LANES), lambda bi, p, i: (bi, 0, 0, k_blk(p))),
            pl.BlockSpec((1, dil, sd, LANES), lambda bi, p, i: (bi, 0, 0, v_blk(p))),
            pl.BlockSpec((2 * n_u, tq_sub, tq_sub + 2 * half), lambda bi, p, i: (p, 0, 0)),
        ],
        out_specs=out_specs,
        out_shape=out_shape,
        compiler_params=_cparams("parallel", "parallel", "arbitrary"),
        name="banded_attn",
    )(slopes, sinks, qkv, qkv, qkv, _banded_bias_table(slopes, half, tq_sub))


def _nbr_kernel(q_ref, k_ref, v_ref, t_ref, o_ref, *, rows_per_tile, n_rows):
    i = pl.program_id(2)
    nkeys = WIN_R * GRID_W
    lane = lax.broadcasted_iota(jnp.int32, (GRID_W, LANES), 1)
    def band(rr):
        r = i * rows_per_tile + rr
        rs = jnp.clip(r - WIN_R // 2, 0, n_rows - WIN_R)
        off = rs - r + (WIN_R - 1)
        return pl.multiple_of(rs * GRID_W, GRID_W), off

    def scores(rr):
        start, _ = band(rr)
        q = q_ref[0, rr * GRID_W:(rr + 1) * GRID_W, :]
        q2 = jnp.concatenate([_head_select(q, lane, 0), _head_select(q, lane, 1)], axis=0)
        return lax.dot_general(q2, k_ref[0, pl.ds(start, nkeys), :], _NT, preferred_element_type=F32)

    def finish(rr, s_all):
        start, off = band(rr)
        prs, dens = [], []
        for hh in range(2):
            s = s_all[hh * GRID_W:(hh + 1) * GRID_W] + t_ref[0, hh, off]
            m = jnp.max(s, axis=-1, keepdims=True)
            pr = jnp.exp2(s - m)
            dens.append(jnp.sum(pr, axis=-1, keepdims=True))
            prs.append(pr.astype(BF16))
        acc = jnp.dot(jnp.concatenate(prs, axis=0), v_ref[0, pl.ds(start, nkeys), :],
                      preferred_element_type=F32)
        o_ref[0, rr * GRID_W:(rr + 1) * GRID_W, :] = jnp.where(
            lane < HEAD_DIM, acc[:GRID_W] / dens[0], acc[GRID_W:] / dens[1]).astype(o_ref.dtype)

    ahead = min(3, rows_per_tile)
    pending = [scores(rr) for rr in range(ahead)]
    for rr in range(rows_per_tile):
        if rr + ahead < rows_per_tile:
            pending.append(scores(rr + ahead))
        finish(rr, pending.pop(0))


def _nbr_bias_table(rel_bias):
    col = jnp.arange(GRID_W)
    cs = jnp.clip(col - WIN_C // 2, 0, GRID_W - WIN_C)
    col_valid = (col[None, :] >= cs[:, None]) & (col[None, :] < cs[:, None] + WIN_C)
    dc_idx = jnp.clip(col[None, :] - col[:, None] + WIN_C - 1, 0, 2 * WIN_C - 2)
    dr = jnp.arange(WIN_R)[:, None] + jnp.arange(WIN_R)[None, :]
    tab = rel_bias.astype(F32)[:, dr] * LOG2E
    tab = tab[..., dc_idx]
    tab = jnp.where(col_valid, tab, NEG)
    tab = jnp.transpose(tab, (0, 1, 3, 2, 4))
    h = tab.shape[0]
    return tab.reshape(h // 2, 2, WIN_R, GRID_W, WIN_R * GRID_W)


def _nbr_call(qkv, table):
    b, s, _ = qkv.shape
    n_tiles = B_HEADS // 2
    n_rows = s // GRID_W
    rows_per_tile = min(32, n_rows)
    tq = rows_per_tile * GRID_W
    kv = lambda off: pl.BlockSpec((1, s, LANES), lambda bi, p, i: (bi, 0, off + p))
    return pl.pallas_call(
        functools.partial(_nbr_kernel, rows_per_tile=rows_per_tile, n_rows=n_rows),
        grid=(b, n_tiles, s // tq),
        in_specs=[
            pl.BlockSpec((1, tq, LANES), lambda bi, p, i: (bi, i, p)),
            kv(n_tiles), kv(2 * n_tiles),
            pl.BlockSpec((1, 2, WIN_R, GRID_W, WIN_R * GRID_W), lambda bi, p, i: (p, 0, 0, 0, 0)),
        ],
        out_specs=pl.BlockSpec((1, tq, LANES), lambda bi, p, i: (bi, i, p)),
        out_shape=jax.ShapeDtypeStruct((b, s, n_tiles * LANES), BF16),
        compiler_params=_cparams("parallel", "parallel", "arbitrary"),
        name="nbr_attn",
    )(qkv, qkv, qkv, table)


_POS_SPLIT = 32


def _diff_kernel(slopes_ref, lam_ref, q_ref, k_ref, vt_ref, kp_ref, g_ref, o_ref,
                 qa_s, m_s, acc_s, s_s, p_s, al_s, rb_s, mx_s, *, t, n_tiles, post_scale):
    h = pl.program_id(1)
    i = pl.program_id(2)
    slope = slopes_ref[h]
    ones_rows = jnp.ones((BF16_ROWS, t), BF16)

    q = q_ref[0]
    lane = lax.broadcasted_iota(jnp.int32, (t, LANES), 1)
    c = jnp.full((t, LANES), -slope, F32)
    c_hi = c.astype(BF16).astype(F32)
    aug = jnp.where(lane < 2, c_hi, jnp.where(lane < 4, c - c_hi, 0.0))
    for mp in range(2):
        qa = jnp.concatenate([_head_select(q, lane, mp).astype(F32), aug], axis=1)
        qa_s[mp] = qa.T.astype(BF16)

    def tile_of(pos):
        if pos == 0:
            return i, None
        after = (pos - 1) >= i
        return pos - 1 + jnp.where(after, 1, 0), after

    m_s[...] = jnp.full(m_s.shape, NEG, F32)
    acc_s[...] = jnp.zeros(acc_s.shape, F32)

    def stage_scores(pos, slot):
        j, after = tile_of(pos)
        kt = k_ref[0, pl.ds(pl.multiple_of(j * t, t), t), :]
        if pos == 0:
            kpos = lax.broadcasted_iota(jnp.int32, (t, t), 0)
            qpos = lax.broadcasted_iota(jnp.int32, (t, t), 1)
            bias = slope * jnp.abs(kpos - qpos).astype(F32)
            rb_s[slot] = jnp.zeros((1, t), F32)
        else:
            ka = jnp.concatenate([kt, kp_ref[jnp.where(after, 0, 1)]], axis=1)
            sgn = jnp.where(after, 1.0, -1.0).astype(F32)
            qrow = lax.broadcasted_iota(jnp.int32, (1, t), 1).astype(F32)
            blk = (jnp.abs(j - i) * t).astype(F32)
            rb_s[slot] = slope * (sgn * qrow - blk)
        for mp in range(2):
            if pos == 0:
                s = jnp.dot(kt, qa_s[mp, :LANES, :], preferred_element_type=F32) - bias
            else:
                s = jnp.dot(ka, qa_s[mp], preferred_element_type=F32)
            s_s[slot, mp] = s
            mx_s[slot, mp] = jnp.max(s, axis=0, keepdims=True)

    def stage_softmax(slot):
        rb = rb_s[slot]
        for mp in range(2):
            m_old = m_s[mp]
            m_new = jnp.maximum(m_old, mx_s[slot, mp] + rb)
            al_s[slot, mp] = jnp.exp2(m_old - m_new)
            p_s[slot, mp] = jnp.exp2(s_s[slot, mp] - (m_new - rb)).astype(BF16)
            m_s[mp] = m_new

    def stage_values(pos, slot):
        j, _ = tile_of(pos)
        va = jnp.concatenate([vt_ref[0, 0, j], ones_rows], axis=0)
        for mp in range(2):
            acc_s[mp] = al_s[slot, mp] * acc_s[mp] + jnp.dot(va, p_s[slot, mp], preferred_element_type=F32)

    for n in range(n_tiles + 2):
        if 2 <= n:
            stage_values(n - 2, n % 2)
        if 1 <= n <= n_tiles:
            stage_softmax(1 - n % 2)
        if n < n_tiles:
            stage_scores(n, n % 2)

    o0 = acc_s[0, :LANES] / acc_s[0, LANES:LANES + 1]
    o1 = acc_s[1, :LANES] / acc_s[1, LANES:LANES + 1]
    o = o0 - lam_ref[0] * o1
    ms = jnp.mean(o * o, axis=0, keepdims=True)
    y = (o * lax.rsqrt(ms + EPS) * g_ref[...]) * post_scale
    o_ref[0] = y.T.astype(o_ref.dtype)


def _key_offset_tiles(t):
    r = jnp.arange(t)
    hi = (r // _POS_SPLIT * _POS_SPLIT).astype(F32)
    lo = (r % _POS_SPLIT).astype(F32)
    tile = jnp.zeros((t, LANES), F32)
    tile = tile.at[:, 0].set(hi).at[:, 1].set(lo).at[:, 2].set(hi).at[:, 3].set(lo)
    return jnp.stack([tile, -tile]).astype(BF16)


def _diff_tile(s):
    return min(512, s)


def _diff_call(qk, vt, slopes, lam, subln, post_scale):
    b, s, _ = qk.shape
    t = _diff_tile(s)
    n_tiles = s // t
    assert t <= _POS_SPLIT * BF16_ROWS
    return pl.pallas_call(
        functools.partial(_diff_kernel, t=t, n_tiles=n_tiles, post_scale=post_scale),
        grid=(b, C_HEADS, n_tiles),
        in_specs=[
            _smem_spec(), _smem_spec(),
            pl.BlockSpec((1, t, LANES), lambda bi, h, i: (bi, i, h)),
            pl.BlockSpec((1, s, LANES), lambda bi, h, i: (bi, 0, C_HEADS + h)),
            pl.BlockSpec((1, 1, n_tiles, LANES, t), lambda bi, h, i: (bi, h, 0, 0, 0)),
            pl.BlockSpec((2, t, LANES), lambda bi, h, i: (0, 0, 0)),
            pl.BlockSpec((LANES, 1), lambda bi, h, i: (0, 0)),
        ],
        out_specs=pl.BlockSpec((1, t, LANES), lambda bi, h, i: (bi, i, h)),
        out_shape=jax.ShapeDtypeStruct((b, s, C_HEADS * LANES), BF16),
        scratch_shapes=[
            pltpu.VMEM((2, 2 * LANES, t), BF16),
            pltpu.VMEM((2, 1, t), F32),
            pltpu.VMEM((2, LANES + BF16_ROWS, t), F32),
            pltpu.VMEM((2, 2, t, t), F32),
            pltpu.VMEM((2, 2, t, t), BF16),
            pltpu.VMEM((2, 2, 1, t), F32),
            pltpu.VMEM((2, 1, t), F32),
            pltpu.VMEM((2, 2, 1, t), F32),
        ],
        compiler_params=_cparams("parallel", "parallel", "parallel"),
        name="diff_attn",
    )(slopes, lam, qk, qk, vt, _key_offset_tiles(t), subln.astype(F32).reshape(LANES, 1))


def _mixer_a(x, g, sc, sh, wqkv, sink):
    d = x.shape[-1]
    nq = A_HEADS * HEAD_DIM
    nkv = A_KV_HEADS * HEAD_DIM

    def dup(wpart):
        wpart = wpart.reshape(d, A_KV_HEADS, 1, HEAD_DIM)
        return jnp.broadcast_to(wpart, (d, A_KV_HEADS, 2, HEAD_DIM)).reshape(d, 2 * nkv).astype(BF16)

    w = jnp.concatenate([_scaled_q_weights(wqkv[:, :nq]), dup(wqkv[:, nq:nq + nkv]),
                         dup(wqkv[:, nq + nkv:])], axis=1)
    qkv, = _qkv_call(x, g, sc, sh, [w])
    q_tiles = nq // LANES
    kv_tiles = 2 * nkv // LANES
    pairs_per_kv = A_HEADS // A_KV_HEADS // 2
    return _banded_call(
        qkv, q_blk=lambda kv: kv, k_blk=lambda kv: q_tiles + kv, v_blk=lambda kv: q_tiles + kv_tiles + kv,
        half=A_WINDOW, dil=1, n_groups=A_KV_HEADS, n_u=pairs_per_kv, slopes=_alibi_slopes(A_HEADS) * LOG2E,
        sinks=sink.astype(F32) * LOG2E, use_sink=True, emit_lse=False), []


def _mixer_b(x, g, sc, sh, wqkv, rel_bias):
    nq = B_HEADS * HEAD_DIM
    w = jnp.concatenate([_scaled_q_weights(wqkv[:, :nq]), wqkv[:, nq:].astype(BF16)], axis=1)
    qkv = _qkv_call(x, g, sc, sh, [w])[0][:, 0]
    return [_nbr_call(qkv, _nbr_bias_table(rel_bias))], []


def _mixer_c(x, g, sc, sh, wqkv, lam, subln, lam_init):
    nq = 2 * C_HEADS * HEAD_DIM
    w = jnp.concatenate([_scaled_q_weights(wqkv[:, :nq]), wqkv[:, nq:2 * nq].astype(BF16)], axis=1)
    qk, vt = _qkv_call(x, g, sc, sh, [w], wt=wqkv[:, 2 * nq:].T.astype(BF16), tile=_diff_tile(x.shape[1]))
    qk = qk[:, 0]
    lf = lam.astype(F32)
    lam_full = jnp.exp(jnp.sum(lf[0] * lf[1])) - jnp.exp(jnp.sum(lf[2] * lf[3])) + lam_init
    return [_diff_call(qk, vt, _alibi_slopes(C_HEADS) * LOG2E, lam_full.reshape(1), subln,
                       1.0 - lam_init)], []


def _mixer_d(x, g, sc, sh, wqkv):
    d = x.shape[-1]
    nh = D_HEADS * HEAD_DIM
    n_tiles = D_HEADS // 2
    outs, lses, ws = [], [], []
    for gi in range(len(D_GROUPS)):
        wg = wqkv[:, gi * 3 * nh:(gi + 1) * 3 * nh]
        ws.append(jnp.concatenate([_scaled_q_weights(wg[:, :nh]), wg[:, nh:].astype(BF16)], axis=1))
    qkvs = _qkv_call(x, g, sc, sh, ws, dils=[dil for _, dil in D_GROUPS])
    for qkv, (win, dil) in zip(qkvs, D_GROUPS):
        o, lse = _banded_call(
            qkv, q_blk=lambda p: p, k_blk=lambda p: n_tiles + p, v_blk=lambda p: 2 * n_tiles + p,
            half=win // (2 * dil), dil=dil, n_groups=n_tiles, n_u=1,
            slopes=_alibi_slopes(D_HEADS) * (LOG2E * dil), sinks=jnp.zeros((D_HEADS,), F32),
            use_sink=False, emit_lse=True)
        outs.append(o)
        lses.append(lse)
    return outs, lses


def _trunk(x, mod, g_mix_pre, g_mix_post, g_mlp_pre, g_mlp_post, mlp_w1, mlp_w2,
           a_wqkv, a_wo, a_sink, b_wqkv, b_wo, b_rel_bias, c_wqkv, c_wo, c_lambda, c_subln, d_wqkv, d_wo):
    depth = mod.shape[0]
    d = x.shape[-1]
    for i in range(depth):
        sh1, sc1, gt1, sh2, sc2, gt2 = [mod[i, :, None, k * d:(k + 1) * d] for k in range(6)]
        kind, occ = i % N_MIXERS, i // N_MIXERS
        if kind == 0:
            (o, lse), wo = _mixer_a(x, g_mix_pre[i], sc1, sh1, a_wqkv[occ], a_sink[occ]), a_wo[occ]
        elif kind == 1:
            (o, lse), wo = _mixer_b(x, g_mix_pre[i], sc1, sh1, b_wqkv[occ], b_rel_bias[occ]), b_wo[occ]
        elif kind == 2:
            (o, lse), wo = _mixer_c(x, g_mix_pre[i], sc1, sh1, c_wqkv[occ], c_lambda[occ], c_subln[occ],
                                    0.8 - 0.6 * math.exp(-0.3 * i)), c_wo[occ]
        else:
            (o, lse), wo = _mixer_d(x, g_mix_pre[i], sc1, sh1, d_wqkv[occ]), d_wo[occ]
        x = _tail_call(o, lse, wo.astype(BF16), x, gt1, g_mix_post[i], g_mlp_pre[i], sc2, sh2,
                       mlp_w1[i].astype(BF16), mlp_w2[i].astype(BF16), gt2, g_mlp_post[i])
    return x


def kernel(x_prompt, x_sample, c_prompt, c_sample, ada_w, ada_b, g_mix_pre, g_mix_post, g_mlp_pre, g_mlp_post, mlp_w1, mlp_w2, a_wqkv, a_wo, a_sink, b_wqkv, b_wo, b_rel_bias, c_wqkv, c_wo, c_lambda, c_subln, d_wqkv, d_wo):
    bp, bs = c_prompt.shape[0], c_sample.shape[0]
    c_all = jnp.concatenate([c_prompt, c_sample], axis=0)
    pad = (-c_all.shape[0]) % 8
    c_all = jnp.pad(c_all, ((0, pad), (0, 0)))
    mod = _adaln_call(c_all, ada_w, ada_b)
    params = (g_mix_pre, g_mix_post, g_mlp_pre, g_mlp_post, mlp_w1, mlp_w2, a_wqkv, a_wo, a_sink,
              b_wqkv, b_wo, b_rel_bias, c_wqkv, c_wo, c_lambda, c_subln, d_wqkv, d_wo)
    y_prompt = _trunk(x_prompt, mod[:, :bp], *params)
    y_sample = _trunk(x_sample, mod[:, bp:bp + bs], *params)
    return (y_prompt, y_sample)
```
